```python
import jax, jax.numpy as jnp
from jax import lax
import numpy as np

D_MODEL = 2048
BATCH = 16
SEQ = 2048
DEPTH = 4

CONV_WIDTH = D_MODEL // 4
CONV_GROUPS = 8
CONV_KERNEL = 31
POOL_WIDTH = D_MODEL // 4
POOL_WINDOWS = (2, 4, 8, 16)
POOL_GROUP = POOL_WIDTH // len(POOL_WINDOWS)
QK_NOPE_DIM = 128
QK_ROPE_DIM = 64
V_HEAD_DIM = 128
MLA_HEADS = (D_MODEL - CONV_WIDTH - POOL_WIDTH) // V_HEAD_DIM
MLA_WIDTH = MLA_HEADS * V_HEAD_DIM
Q_LORA_RANK = D_MODEL // 4
KV_LORA_RANK = D_MODEL // 8
MIX_WIDTH = MLA_WIDTH + CONV_WIDTH + POOL_WIDTH
IN_COLS = Q_LORA_RANK + KV_LORA_RANK + QK_ROPE_DIM + 2 * CONV_WIDTH + POOL_WIDTH
D_FF = ((8 * D_MODEL // 3 + 255) // 256) * 256
FFN_CONV_KERNEL = 3
ROPE_THETA = 10000.0
Q_BLOCK = 128
LN_EPS = 1e-5
RMS_EPS = 1e-6
DEEPNORM_ALPHA = (2.0 * DEPTH) ** 0.25
DEEPNORM_BETA = (8.0 * DEPTH) ** -0.25

kernel_name = "hymba_style_mla_conformer_pool_hybrid"


def layer_norm(x, g, b):
    xf = x.astype(jnp.float32)
    mu = jnp.mean(xf, axis=-1, keepdims=True)
    var = jnp.mean(jnp.square(xf - mu), axis=-1, keepdims=True)
    return ((xf - mu) * lax.rsqrt(var + LN_EPS) * g.astype(jnp.float32) + b.astype(jnp.float32)).astype(x.dtype)


def rms_norm(x, g):
    xf = x.astype(jnp.float32)
    ms = jnp.mean(jnp.square(xf), axis=-1, keepdims=True)
    return (xf * lax.rsqrt(ms + RMS_EPS) * g.astype(jnp.float32)).astype(x.dtype)


def causal_dwconv(x, w, b):
    k, c = w.shape
    y = lax.conv_general_dilated(
        x, w[:, None, :].astype(x.dtype), window_strides=(1,), padding=[(k - 1, 0)],
        dimension_numbers=("NWC", "WIO", "NWC"), feature_group_count=c)
    return y + b.astype(x.dtype)


def rope_cos_sin(positions):
    inv = 1.0 / (ROPE_THETA ** (jnp.arange(0, QK_ROPE_DIM, 2, dtype=jnp.float32) / QK_ROPE_DIM))
    ang = positions.astype(jnp.float32)[..., None] * inv
    return jnp.cos(ang), jnp.sin(ang)


def apply_rope(x, cos, sin):
    xf = x.astype(jnp.float32)
    x1, x2 = jnp.split(xf, 2, axis=-1)
    return jnp.concatenate([x1 * cos - x2 * sin, x1 * sin + x2 * cos], axis=-1).astype(x.dtype)


def mla_mixer(c_q, c_kv, k_rope, q_norm_g, w_uq, kv_norm_g, w_ukv, cos, sin):
    bsz, seq, _ = c_q.shape
    q = jnp.einsum("bsr,rhd->bshd", rms_norm(c_q, q_norm_g), w_uq)
    q_nope = q[..., :QK_NOPE_DIM]
    q_rope = apply_rope(q[..., QK_NOPE_DIM:], cos[:, :, None, :], sin[:, :, None, :])
    kv = jnp.einsum("bsr,rhd->bshd", rms_norm(c_kv, kv_norm_g), w_ukv)
    k_nope = kv[..., :QK_NOPE_DIM]
    v = kv[..., QK_NOPE_DIM:]
    k_rope = apply_rope(k_rope, cos, sin)
    qb = min(Q_BLOCK, seq)
    nb = seq // qb
    qn_blocks = q_nope.reshape(bsz, nb, qb, MLA_HEADS, QK_NOPE_DIM).transpose(1, 0, 2, 3, 4)
    qr_blocks = q_rope.reshape(bsz, nb, qb, MLA_HEADS, QK_ROPE_DIM).transpose(1, 0, 2, 3, 4)
    key_idx = jnp.arange(seq)
    scale = (QK_NOPE_DIM + QK_ROPE_DIM) ** -0.5
    neg = jnp.finfo(jnp.float32).min

    def attend(args):
        qn, qr, start = args
        s = (jnp.einsum("bqhd,bkhd->bhqk", qn, k_nope, preferred_element_type=jnp.float32)
             + jnp.einsum("bqhr,bkr->bhqk", qr, k_rope, preferred_element_type=jnp.float32)) * scale
        q_idx = start + jnp.arange(qb)
        s = jnp.where(key_idx[None, :] <= q_idx[:, None], s, neg)
        p = jax.nn.softmax(s, axis=-1)
        return jnp.einsum("bhqk,bkhd->bqhd", p.astype(v.dtype), v)

    out = lax.map(attend, (qn_blocks, qr_blocks, jnp.arange(nb) * qb))
    return out.transpose(1, 0, 2, 3, 4).reshape(bsz, seq, MLA_WIDTH)


def conformer_conv_mixer(u, conv_w, conv_b, ln_g, ln_b):
    a, g = jnp.split(u, 2, axis=-1)
    h = a * jax.nn.sigmoid(g)
    h = causal_dwconv(h, conv_w, conv_b)
    h = layer_norm(h, ln_g, ln_b)
    return jax.nn.silu(h)


def pool_mixer(u, w_pool, scale):
    bsz, seq, c = u.shape
    uf = u.astype(jnp.float32)
    cs = jnp.concatenate([jnp.zeros((bsz, 1, c), jnp.float32), lax.cumsum(uf, axis=1)], axis=1)
    t = jnp.arange(seq)
    outs = []
    for gi, w in enumerate(POOL_WINDOWS):
        sl = slice(gi * POOL_GROUP, (gi + 1) * POOL_GROUP)
        lo = jnp.maximum(t + 1 - w, 0)
        win_sum = cs[:, 1:, sl] - cs[:, lo, sl]
        cnt = (t + 1 - lo).astype(jnp.float32)[None, :, None]
        outs.append(win_sum / cnt - uf[:, :, sl])
    d = jnp.stack(outs, axis=2).astype(u.dtype)
    y = jnp.einsum("bsgc,gcd->bsgd", d, w_pool).reshape(bsz, seq, c)
    return y * scale


def setup_inputs(seed: int = 0) -> dict:
    key = jax.random.key(seed)
    ks = jax.random.split(key, 24)
    f32 = jnp.float32

    def nrm(k, shape, s):
        return jax.random.normal(k, shape, f32) * s

    def gain(k, shape):
        return 1.0 + 0.02 * jax.random.normal(k, shape, f32)

    L = DEPTH
    x = jax.random.normal(ks[0], (BATCH, SEQ, D_MODEL), f32)
    positions = jnp.broadcast_to(jnp.arange(SEQ, dtype=jnp.int32)[None, :], (BATCH, SEQ))
    return {
        "x": x,
        "positions": positions,
        "ln_in_g": gain(ks[1], (D_MODEL,)),
        "ln_in_b": nrm(ks[2], (D_MODEL,), 0.02),
        "w_in": nrm(ks[3], (L, D_MODEL, IN_COLS), D_MODEL ** -0.5),
        "q_norm_g": gain(ks[4], (L, Q_LORA_RANK)),
        "w_uq": nrm(ks[5], (L, Q_LORA_RANK, MLA_HEADS, QK_NOPE_DIM + QK_ROPE_DIM), Q_LORA_RANK ** -0.5),
        "kv_norm_g": gain(ks[6], (L, KV_LORA_RANK)),
        "w_ukv": nrm(ks[7], (L, KV_LORA_RANK, MLA_HEADS, QK_NOPE_DIM + V_HEAD_DIM), KV_LORA_RANK ** -0.5),
        "conv_w": nrm(ks[8], (L, CONV_KERNEL, CONV_WIDTH), CONV_KERNEL ** -0.5),
        "conv_b": nrm(ks[9], (L, CONV_WIDTH), 0.02),
        "conv_ln_g": gain(ks[10], (L, CONV_WIDTH)),
        "conv_ln_b": nrm(ks[11], (L, CONV_WIDTH), 0.02),
        "w_pool": nrm(ks[12], (L, len(POOL_WINDOWS), POOL_GROUP, POOL_GROUP), POOL_GROUP ** -0.5),
        "pool_scale": gain(ks[13], (L, POOL_WIDTH)),
        "w_out": nrm(ks[14], (L, MIX_WIDTH, D_MODEL), DEEPNORM_BETA * MIX_WIDTH ** -0.5),
        "ln1_g": gain(ks[15], (L, D_MODEL)),
        "ln1_b": nrm(ks[16], (L, D_MODEL), 0.02),
        "w_up": nrm(ks[17], (L, D_MODEL, 2 * D_FF), D_MODEL ** -0.5),
        "ffn_conv_w": nrm(ks[18], (L, FFN_CONV_KERNEL, 2 * D_FF), FFN_CONV_KERNEL ** -0.5),
        "ffn_conv_b": nrm(ks[19], (L, 2 * D_FF), 0.02),
        "w_down": nrm(ks[20], (L, D_FF, D_MODEL), DEEPNORM_BETA * D_FF ** -0.5),
        "ln2_g": gain(ks[21], (L, D_MODEL)),
        "ln2_b": nrm(ks[22], (L, D_MODEL), 0.02),
    }


def reference(x, positions, ln_in_g, ln_in_b, w_in, q_norm_g, w_uq, kv_norm_g, w_ukv,
              conv_w, conv_b, conv_ln_g, conv_ln_b, w_pool, pool_scale, w_out, ln1_g, ln1_b,
              w_up, ffn_conv_w, ffn_conv_b, w_down, ln2_g, ln2_b):
    cos, sin = rope_cos_sin(positions)
    x = layer_norm(x, ln_in_g, ln_in_b)
    o1 = Q_LORA_RANK
    o2 = o1 + KV_LORA_RANK
    o3 = o2 + QK_ROPE_DIM
    o4 = o3 + 2 * CONV_WIDTH
    for l in range(DEPTH):
        h = jnp.einsum("bsd,dc->bsc", x, w_in[l])
        c_q, c_kv, k_rope = h[..., :o1], h[..., o1:o2], h[..., o2:o3]
        u_conv, u_pool = h[..., o3:o4], h[..., o4:]
        y_mla = mla_mixer(c_q, c_kv, k_rope, q_norm_g[l], w_uq[l], kv_norm_g[l], w_ukv[l], cos, sin)
        y_conv = conformer_conv_mixer(u_conv, conv_w[l], conv_b[l], conv_ln_g[l], conv_ln_b[l])
        y_pool = pool_mixer(u_pool, w_pool[l], pool_scale[l])
        mixed = jnp.concatenate([y_mla, y_conv, y_pool], axis=-1)
        y = jnp.einsum("bsc,cd->bsd", mixed, w_out[l])
        x = layer_norm(DEEPNORM_ALPHA * x + y, ln1_g[l], ln1_b[l])
        up = jnp.einsum("bsd,df->bsf", x, w_up[l])
        up = causal_dwconv(up, ffn_conv_w[l], ffn_conv_b[l])
        a, g = jnp.split(up, 2, axis=-1)
        y = jnp.einsum("bsf,fd->bsd", a * jax.nn.silu(g), w_down[l])
        x = layer_norm(DEEPNORM_ALPHA * x + y, ln2_g[l], ln2_b[l])
    return x
```

```python
import functools

import jax
import jax.numpy as jnp
from jax import lax
from jax.experimental import pallas as pl
from jax.experimental.pallas import tpu as pltpu

F32 = jnp.float32
BF16 = jnp.bfloat16

QK_NOPE_DIM = 128
QK_ROPE_DIM = 64
V_HEAD_DIM = 128
POOL_WINDOWS = (2, 4, 8, 16)
ROPE_THETA = 10000.0
LN_EPS = 1e-5
RMS_EPS = 1e-6

LANES = 128
SUBLANES = 8
VMEM_LIMIT_BYTES = 56 * 1024 * 1024


def _params(*sem):
    return pltpu.CompilerParams(dimension_semantics=sem, vmem_limit_bytes=VMEM_LIMIT_BYTES)


def _resident(shape):
    return pl.BlockSpec(shape, lambda *_: (0,) * len(shape), pipeline_mode=pl.Buffered(1))


def _layer_norm(z, g, b):
    mu = jnp.mean(z, axis=-1, keepdims=True)
    zc = z - mu
    var = jnp.mean(zc * zc, axis=-1, keepdims=True)
    return zc * lax.rsqrt(var + LN_EPS) * g + b


def _rms_norm(z, g):
    ms = jnp.mean(z * z, axis=-1, keepdims=True)
    return z * lax.rsqrt(ms + RMS_EPS) * g


def _sigmoid(z):
    return 1.0 / (1.0 + jnp.exp(-z))


def _rope_table_kernel(pos_ref, inv_ref, cos_ref, sin_ref):
    ang = pos_ref[...].astype(F32) * inv_ref[...]
    cos_ref[...] = jnp.cos(ang)
    sin_ref[...] = jnp.sin(ang)


def _rope_table(pos_col, inv_row, tm):
    rows = pos_col.shape[0]
    return pl.pallas_call(
        _rope_table_kernel,
        grid=(rows // tm,),
        in_specs=[pl.BlockSpec((tm, 1), lambda i: (i, 0)),
                  pl.BlockSpec((1, LANES), lambda i: (0, 0))],
        out_specs=[pl.BlockSpec((tm, LANES), lambda i: (i, 0))] * 2,
        out_shape=[jax.ShapeDtypeStruct((rows, LANES), F32)] * 2,
        compiler_params=_params("arbitrary"),
        name="rope_table",
    )(pos_col, inv_row)


def _ln_kernel(x_ref, g_ref, b_ref, o_ref):
    o_ref[...] = _layer_norm(x_ref[...], g_ref[...], b_ref[...])


def _input_ln(x2, g, b, tm):
    rows, d = x2.shape
    return pl.pallas_call(
        _ln_kernel,
        grid=(rows // tm,),
        in_specs=[pl.BlockSpec((tm, d), lambda i: (i, 0)),
                  pl.BlockSpec((1, d), lambda i: (0, 0)),
                  pl.BlockSpec((1, d), lambda i: (0, 0))],
        out_specs=pl.BlockSpec((tm, d), lambda i: (i, 0)),
        out_shape=jax.ShapeDtypeStruct((rows, d), F32),
        compiler_params=_params("arbitrary"),
        name="input_ln",
    )(x2, g, b)


def _inproj_kernel(x_ref, win_ref, wuq_ref, wukv_ref, qg_ref, kvg_ref, cos_ref, sin_ref,
                   q_ref, k_ref, v_ref, uc_ref, up_ref, *, heads, ql, kvl, cwid2, scale):
    dn, dv = QK_NOPE_DIM, V_HEAD_DIM
    o_kr = ql + kvl
    o_uc = o_kr + 2 * LANES
    o_up = o_uc + cwid2
    xb = x_ref[...].astype(BF16)
    cos = cos_ref[...]
    sin = sin_ref[...]

    uc_ref[...] = jnp.dot(xb, win_ref[:, o_uc:o_up], preferred_element_type=F32)
    up_ref[...] = jnp.dot(xb, win_ref[:, o_up:], preferred_element_type=F32)

    hk = jnp.dot(xb, win_ref[:, o_kr:o_uc], preferred_element_type=F32)
    kr = (hk[:, :LANES] * cos + hk[:, LANES:] * sin).astype(BF16)

    cq = jnp.dot(xb, win_ref[:, :ql], preferred_element_type=F32)
    cqn = _rms_norm(cq, qg_ref[...]).astype(BF16)
    q = jnp.dot(cqn, wuq_ref[...], preferred_element_type=F32)
    n_nope = heads * dn
    n_rope = heads * QK_ROPE_DIM
    lane = lax.broadcasted_iota(jnp.int32, (1, LANES), 1)
    for p in range(n_rope // LANES):
        a = q[:, n_nope + p * LANES:n_nope + (p + 1) * LANES]
        r = q[:, n_nope + n_rope + p * LANES:n_nope + n_rope + (p + 1) * LANES]
        roped = (a * cos + r * sin) * scale
        for s in range(2):
            h = 2 * p + s
            keep = (lane < QK_ROPE_DIM) if s == 0 else (lane >= QK_ROPE_DIM)
            q_ref[:, h * 2 * dn + dn:(h + 1) * 2 * dn] = jnp.where(keep, roped, 0.0).astype(BF16)
    for h in range(heads):
        q_ref[:, h * 2 * dn:h * 2 * dn + dn] = (q[:, h * dn:(h + 1) * dn] * scale).astype(BF16)

    ckv = jnp.dot(xb, win_ref[:, ql:o_kr], preferred_element_type=F32)
    ckvn = _rms_norm(ckv, kvg_ref[...]).astype(BF16)
    kv = jnp.dot(ckvn, wukv_ref[...], preferred_element_type=F32)
    for h in range(heads):
        k_ref[:, h * 2 * dn:h * 2 * dn + dn] = kv[:, h * dn:(h + 1) * dn].astype(BF16)
        k_ref[:, h * 2 * dn + dn:(h + 1) * 2 * dn] = kr
    v_ref[...] = kv[:, n_nope:].astype(BF16)


def _inproj(x2, win, wuq, wukv, qg, kvg, cos, sin, *, heads, cwid2, pwid, tm):
    rows, d = x2.shape
    ql = qg.shape[1]
    kvl = kvg.shape[1]
    dn, dv = QK_NOPE_DIM, V_HEAD_DIM
    scale = (QK_NOPE_DIM + QK_ROPE_DIM) ** -0.5
    row = lambda w: pl.BlockSpec((tm, w), lambda i: (i, 0))
    return pl.pallas_call(
        functools.partial(_inproj_kernel, heads=heads, ql=ql, kvl=kvl, cwid2=cwid2, scale=scale),
        grid=(rows // tm,),
        in_specs=[row(d), _resident(win.shape), _resident(wuq.shape), _resident(wukv.shape),
                  _resident(qg.shape), _resident(kvg.shape), row(LANES), row(LANES)],
        out_specs=[row(heads * 2 * dn), row(heads * 2 * dn), row(heads * dv), row(cwid2), row(pwid)],
        out_shape=[jax.ShapeDtypeStruct((rows, heads * 2 * dn), BF16),
                   jax.ShapeDtypeStruct((rows, heads * 2 * dn), BF16),
                   jax.ShapeDtypeStruct((rows, heads * dv), BF16),
                   jax.ShapeDtypeStruct((rows, cwid2), F32),
                   jax.ShapeDtypeStruct((rows, pwid), F32)],
        compiler_params=_params("arbitrary"),
        name="inproj",
    )(x2, win, wuq, wukv, qg, kvg, cos, sin)


def _attn_kernel(q_ref, k_ref, v_ref, o_ref, *, tq):
    seq = q_ref.shape[1]
    nt = (((1,), (1,)), ((), ()))
    row = lax.broadcasted_iota(jnp.int32, (tq, tq), 0)
    col = lax.broadcasted_iota(jnp.int32, (tq, tq), 1)
    neg = jnp.finfo(F32).min
    for iq in range(seq // tq):
        k0 = iq * tq
        q = q_ref[0, k0:k0 + tq, :]
        sd = lax.dot_general(q, k_ref[0, k0:k0 + tq, :], nt, preferred_element_type=F32)
        sd = jnp.where(col <= row, sd, neg)
        m = jnp.max(sd, axis=-1, keepdims=True)
        if k0 > 0:
            sp = lax.dot_general(q, k_ref[0, :k0, :], nt, preferred_element_type=F32)
            m = jnp.maximum(m, jnp.max(sp, axis=-1, keepdims=True))
            pp = jnp.exp(sp - m)
            l = jnp.sum(pp, axis=-1, keepdims=True)
            acc = jnp.dot(pp.astype(BF16), v_ref[0, :k0, :], preferred_element_type=F32)
        pd = jnp.exp(sd - m)
        ld = jnp.sum(pd, axis=-1, keepdims=True)
        accd = jnp.dot(pd.astype(BF16), v_ref[0, k0:k0 + tq, :], preferred_element_type=F32)
        if k0 > 0:
            l = l + ld
            acc = acc + accd
        else:
            l, acc = ld, accd
        o_ref[0, k0:k0 + tq, :] = (acc / l).astype(BF16)


def _attention(q3, k3, v3, *, heads, tq):
    bsz, seq, _ = q3.shape
    dqk = 2 * QK_NOPE_DIM
    dv = V_HEAD_DIM
    return pl.pallas_call(
        functools.partial(_attn_kernel, tq=tq),
        grid=(bsz, heads),
        in_specs=[pl.BlockSpec((1, seq, dqk), lambda b, h: (b, 0, h)),
                  pl.BlockSpec((1, seq, dqk), lambda b, h: (b, 0, h)),
                  pl.BlockSpec((1, seq, dv), lambda b, h: (b, 0, h))],
        out_specs=pl.BlockSpec((1, seq, dv), lambda b, h: (b, 0, h)),
        out_shape=jax.ShapeDtypeStruct((bsz, seq, heads * dv), BF16),
        compiler_params=_params("arbitrary", "arbitrary"),
        name="attention",
    )(q3, k3, v3)


CONV_HALO = 32
POOL_HALO = 16
CONV_CHUNK = 64


def _mixer_kernel(uc_ref, up_ref, cw_ref, cb_ref, lg_ref, lb_ref, wp_ref, ps_ref,
                  yc_ref, yp_ref, hbuf, hsh, pbuf, *, tm, cwid, taps):
    st = pl.program_id(1)

    @pl.when(st == 0)
    def _():
        hbuf[0:CONV_HALO, :] = jnp.zeros((CONV_HALO, cwid), F32)
        pbuf[0:POOL_HALO, :] = jnp.zeros((POOL_HALO, pbuf.shape[1]), F32)

    a = uc_ref[0, :, :cwid]
    g = uc_ref[0, :, cwid:]
    hbuf[CONV_HALO:CONV_HALO + tm, :] = a * _sigmoid(g)
    hsh[0] = hbuf[...]
    span = tm + CONV_HALO - SUBLANES
    for sh in range(1, SUBLANES):
        hsh[sh, 0:span, :] = hbuf[sh:sh + span, :]

    cb = cb_ref[...]
    lg = lg_ref[...]
    lb = lb_ref[...]
    base = CONV_HALO - (taps - 1)

    def chunk(c, carry):
        r0 = pl.multiple_of(c * CONV_CHUNK, CONV_CHUNK)
        acc = jnp.zeros((CONV_CHUNK, cwid), F32)
        for k in range(taps):
            off = base + k
            rows = pl.ds(r0 + (off // SUBLANES) * SUBLANES, CONV_CHUNK)
            acc = acc + cw_ref[k:k + 1, :] * hsh[off % SUBLANES, rows, :]
        y = _layer_norm(acc + cb, lg, lb)
        yc_ref[0, pl.ds(r0, CONV_CHUNK), :] = (y * _sigmoid(y)).astype(BF16)
        return carry

    lax.fori_loop(0, tm // CONV_CHUNK, chunk, 0)
    hbuf[0:CONV_HALO, :] = hbuf[tm:tm + CONV_HALO, :]

    pbuf[POOL_HALO:POOL_HALO + tm, :] = up_ref[0]
    t = st * tm + lax.broadcasted_iota(jnp.int32, (tm, 1), 0)
    for gi, w in enumerate(POOL_WINDOWS):
        lo, hi = gi * LANES, (gi + 1) * LANES
        ws = pbuf[POOL_HALO:POOL_HALO + tm, lo:hi]
        for i in range(1, w):
            ws = ws + pbuf[POOL_HALO - i:POOL_HALO - i + tm, lo:hi]
        cnt = jnp.minimum(t + 1, w).astype(F32)
        d = ws / cnt - pbuf[POOL_HALO:POOL_HALO + tm, lo:hi]
        y = jnp.dot(d.astype(BF16), wp_ref[gi], preferred_element_type=F32)
        yp_ref[0, :, lo:hi] = (y * ps_ref[:, lo:hi]).astype(BF16)
    pbuf[0:POOL_HALO, :] = pbuf[tm:tm + POOL_HALO, :]


def _mixers(uc3, up3, cw, cb, lg, lb, wp, ps, *, tm):
    bsz, seq, cwid2 = uc3.shape
    cwid = cwid2 // 2
    pwid = up3.shape[2]
    taps = cw.shape[0]
    assert taps - 1 <= CONV_HALO and max(POOL_WINDOWS) - 1 <= POOL_HALO
    assert pwid == len(POOL_WINDOWS) * LANES and tm % CONV_CHUNK == 0
    return pl.pallas_call(
        functools.partial(_mixer_kernel, tm=tm, cwid=cwid, taps=taps),
        grid=(bsz, seq // tm),
        in_specs=[pl.BlockSpec((1, tm, cwid2), lambda b, s: (b, s, 0)),
                  pl.BlockSpec((1, tm, pwid), lambda b, s: (b, s, 0)),
                  _resident(cw.shape), _resident(cb.shape), _resident(lg.shape), _resident(lb.shape),
                  _resident(wp.shape), _resident(ps.shape)],
        out_specs=[pl.BlockSpec((1, tm, cwid), lambda b, s: (b, s, 0)),
                   pl.BlockSpec((1, tm, pwid), lambda b, s: (b, s, 0))],
        out_shape=[jax.ShapeDtypeStruct((bsz, seq, cwid), BF16),
                   jax.ShapeDtypeStruct((bsz, seq, pwid), BF16)],
        scratch_shapes=[pltpu.VMEM((tm + CONV_HALO, cwid), F32),
                        pltpu.VMEM((SUBLANES, tm + CONV_HALO, cwid), F32),
                        pltpu.VMEM((tm + POOL_HALO, pwid), F32)],
        compiler_params=_params("arbitrary", "arbitrary"),
        name="mixers",
    )(uc3, up3, cw, cb, lg, lb, wp, ps)


def _outproj_kernel(ym_ref, yc_ref, yp_ref, x_ref, w_ref, g_ref, b_ref, o_ref, *, alpha):
    mixed = jnp.concatenate([ym_ref[...], yc_ref[...], yp_ref[...]], axis=1)
    y = jnp.dot(mixed, w_ref[...], preferred_element_type=F32)
    o_ref[...] = _layer_norm(alpha * x_ref[...] + y, g_ref[...], b_ref[...])


def _outproj(ym, yc, yp, x2, w, g, b, *, alpha, tm):
    rows, d = x2.shape
    row = lambda wd: pl.BlockSpec((tm, wd), lambda i: (i, 0))
    return pl.pallas_call(
        functools.partial(_outproj_kernel, alpha=alpha),
        grid=(rows // tm,),
        in_specs=[row(ym.shape[1]), row(yc.shape[1]), row(yp.shape[1]), row(d),
                  _resident(w.shape), _resident(g.shape), _resident(b.shape)],
        out_specs=row(d),
        out_shape=jax.ShapeDtypeStruct((rows, d), F32),
        compiler_params=_params("arbitrary"),
        name="outproj",
    )(ym, yc, yp, x2, w, g, b)


FFN_HALO = 8


def _ffn_kernel(x_ref, wa_ref, wg_ref, cwa_ref, cwg_ref, cba_ref, cbg_ref, wd_ref, g_ref, b_ref,
                o_ref, xb_ref, ua_buf, ug_buf, halo_a, halo_g, *, tm, tiles_per_seq, alpha):
    i = pl.program_id(0)
    j = pl.program_id(1)
    nj = pl.num_programs(1)

    @pl.when(j == 0)
    def _():
        xb_ref[...] = x_ref[...].astype(BF16)

    xb = xb_ref[...]
    seq_start = (i % tiles_per_seq) == 0

    def conv_branch(w_ref, cw_ref, cb_ref, buf, halo):
        u = jnp.dot(xb, w_ref[...], preferred_element_type=F32)
        @pl.when(seq_start)
        def _():
            buf[0:FFN_HALO, :] = jnp.zeros((FFN_HALO, u.shape[1]), F32)

        @pl.when(jnp.logical_not(seq_start))
        def _():
            buf[0:FFN_HALO, :] = halo[j]

        buf[FFN_HALO:FFN_HALO + tm, :] = u
        halo[j] = u[tm - FFN_HALO:, :]
        return (cw_ref[2:3, :] * u
                + cw_ref[1:2, :] * buf[FFN_HALO - 1:FFN_HALO - 1 + tm, :]
                + cw_ref[0:1, :] * buf[FFN_HALO - 2:FFN_HALO - 2 + tm, :]
                + cb_ref[...])

    ca = conv_branch(wa_ref, cwa_ref, cba_ref, ua_buf, halo_a)
    cg = conv_branch(wg_ref, cwg_ref, cbg_ref, ug_buf, halo_g)
    hid = (ca * (cg * _sigmoid(cg))).astype(BF16)
    y = jnp.dot(hid, wd_ref[...], preferred_element_type=F32)

    @pl.when(j == 0)
    def _():
        o_ref[...] = y

    @pl.when(j > 0)
    def _():
        o_ref[...] += y

    @pl.when(j == nj - 1)
    def _():
        o_ref[...] = _layer_norm(alpha * x_ref[...] + o_ref[...], g_ref[...], b_ref[...])


def _ffn(x2, wup, cw, cb, wdn, g, b, *, seq, alpha, tm, tf):
    rows, d = x2.shape
    dff = wdn.shape[0]
    nj = dff // tf
    assert seq % tm == 0 and dff % tf == 0 and cw.shape[0] - 1 <= FFN_HALO
    return pl.pallas_call(
        functools.partial(_ffn_kernel, tm=tm, tiles_per_seq=seq // tm, alpha=alpha),
        grid=(rows // tm, nj),
        in_specs=[pl.BlockSpec((tm, d), lambda i, j: (i, 0)),
                  pl.BlockSpec((d, tf), lambda i, j: (0, j)),
                  pl.BlockSpec((d, tf), lambda i, j: (0, j + nj)),
                  pl.BlockSpec((cw.shape[0], tf), lambda i, j: (0, j)),
                  pl.BlockSpec((cw.shape[0], tf), lambda i, j: (0, j + nj)),
                  pl.BlockSpec((1, tf), lambda i, j: (0, j)),
                  pl.BlockSpec((1, tf), lambda i, j: (0, j + nj)),
                  pl.BlockSpec((tf, d), lambda i, j: (j, 0)),
                  _resident(g.shape), _resident(b.shape)],
        out_specs=pl.BlockSpec((tm, d), lambda i, j: (i, 0)),
        out_shape=jax.ShapeDtypeStruct((rows, d), F32),
        scratch_shapes=[pltpu.VMEM((tm, d), BF16),
                        pltpu.VMEM((tm + FFN_HALO, tf), F32),
                        pltpu.VMEM((tm + FFN_HALO, tf), F32),
                        pltpu.VMEM((nj, FFN_HALO, tf), F32),
                        pltpu.VMEM((nj, FFN_HALO, tf), F32)],
        compiler_params=_params("arbitrary", "arbitrary"),
        name="ffn",
    )(x2, wup, wup, cw, cw, cb, cb, wdn, g, b)


def _rot_half(w):
    half = w.shape[-1] // 2
    return jnp.concatenate([-w[..., half:], w[..., :half]], axis=-1)


def _prep_in_weights(w_in_l, ql, kvl, cwid2):
    o_kr = ql + kvl
    o_uc = o_kr + QK_ROPE_DIM
    kr = w_in_l[:, o_kr:o_uc]
    krr = _rot_half(kr)
    return jnp.concatenate([w_in_l[:, :o_kr], kr, kr, krr, krr, w_in_l[:, o_uc:]], axis=1).astype(BF16)


def _prep_uq(w_uq_l):
    r, heads, _ = w_uq_l.shape
    nope = w_uq_l[:, :, :QK_NOPE_DIM].reshape(r, heads * QK_NOPE_DIM)
    rope = w_uq_l[:, :, QK_NOPE_DIM:]
    return jnp.concatenate([nope, rope.reshape(r, -1), _rot_half(rope).reshape(r, -1)], axis=1).astype(BF16)


def _prep_ukv(w_ukv_l):
    r, heads, _ = w_ukv_l.shape
    kn = w_ukv_l[:, :, :QK_NOPE_DIM].reshape(r, heads * QK_NOPE_DIM)
    v = w_ukv_l[:, :, QK_NOPE_DIM:].reshape(r, heads * V_HEAD_DIM)
    return jnp.concatenate([kn, v], axis=1).astype(BF16)


def _pick(n, pref):
    t = min(n, pref)
    assert n % t == 0
    return t


def kernel(x, positions, ln_in_g, ln_in_b, w_in, q_norm_g, w_uq, kv_norm_g, w_ukv, conv_w, conv_b, conv_ln_g, conv_ln_b, w_pool, pool_scale, w_out, ln1_g, ln1_b, w_up, ffn_conv_w, ffn_conv_b, w_down, ln2_g, ln2_b):
    bsz, seq, d = x.shape
    depth = w_in.shape[0]
    rows = bsz * seq
    heads = w_uq.shape[2]
    ql = q_norm_g.shape[1]
    kvl = kv_norm_g.shape[1]
    cwid = conv_w.shape[2]
    pwid = pool_scale.shape[1]
    alpha = (2.0 * depth) ** 0.25

    inv = 1.0 / (ROPE_THETA ** (jnp.arange(0, QK_ROPE_DIM, 2, dtype=F32) / QK_ROPE_DIM))
    inv_row = jnp.tile(inv, LANES // inv.shape[0])[None, :]
    cos, sin = _rope_table(positions.reshape(rows, 1), inv_row, _pick(rows, 1024))

    xs = _input_ln(x.reshape(rows, d), ln_in_g[None, :], ln_in_b[None, :], _pick(rows, 512))

    for l in range(depth):
        q2, k2, v2, uc, up = _inproj(
            xs, _prep_in_weights(w_in[l], ql, kvl, 2 * cwid), _prep_uq(w_uq[l]), _prep_ukv(w_ukv[l]),
            q_norm_g[l][None, :], kv_norm_g[l][None, :], cos, sin,
            heads=heads, cwid2=2 * cwid, pwid=pwid, tm=_pick(rows, 256))
        ym = _attention(q2.reshape(bsz, seq, -1), k2.reshape(bsz, seq, -1), v2.reshape(bsz, seq, -1),
                        heads=heads, tq=_pick(seq, 256))
        yc, yp = _mixers(uc.reshape(bsz, seq, -1), up.reshape(bsz, seq, -1),
                         conv_w[l], conv_b[l][None, :], conv_ln_g[l][None, :], conv_ln_b[l][None, :],
                         w_pool[l].astype(BF16), pool_scale[l][None, :], tm=_pick(seq, 512))
        xs = _outproj(ym.reshape(rows, -1), yc.reshape(rows, -1), yp.reshape(rows, -1), xs,
                      w_out[l].astype(BF16), ln1_g[l][None, :], ln1_b[l][None, :],
                      alpha=alpha, tm=_pick(rows, 512))
        xs = _ffn(xs, w_up[l].astype(BF16), ffn_conv_w[l], ffn_conv_b[l][None, :], w_down[l].astype(BF16),
                  ln2_g[l][None, :], ln2_b[l][None, :],
                  seq=seq, alpha=alpha, tm=_pick(seq, 512), tf=_pick(w_down.shape[1], 512))
    return xs.reshape(bsz, seq, d)
```

```python
import functools

import jax
import jax.numpy as jnp
from jax import lax
from jax.experimental import pallas as pl
from jax.experimental.pallas import tpu as pltpu

F32 = jnp.float32
BF16 = jnp.bfloat16

QK_NOPE_DIM = 128
QK_ROPE_DIM = 64
V_HEAD_DIM = 128
POOL_WINDOWS = (2, 4, 8, 16)
ROPE_THETA = 10000.0
LN_EPS = 1e-5
RMS_EPS = 1e-6

LANES = 128
SUBLANES = 8
VMEM_LIMIT_BYTES = 56 * 1024 * 1024


def _params(*sem):
    return pltpu.CompilerParams(dimension_semantics=sem, vmem_limit_bytes=VMEM_LIMIT_BYTES)


def _resident(shape):
    return pl.BlockSpec(shape, lambda *_: (0,) * len(shape), pipeline_mode=pl.Buffered(1))


def _layer_norm(z, g, b):
    mu = jnp.mean(z, axis=-1, keepdims=True)
    zc = z - mu
    var = jnp.mean(zc * zc, axis=-1, keepdims=True)
    return zc * lax.rsqrt(var + LN_EPS) * g + b


def _rms_norm(z, g):
    ms = jnp.mean(z * z, axis=-1, keepdims=True)
    return z * lax.rsqrt(ms + RMS_EPS) * g


def _sigmoid(z):
    return 1.0 / (1.0 + jnp.exp(-z))


def _rope_table_kernel(pos_ref, inv_ref, cos_ref, sin_ref):
    ang = pos_ref[...].astype(F32) * inv_ref[...]
    cos_ref[...] = jnp.cos(ang)
    sin_ref[...] = jnp.sin(ang)


def _rope_table(pos_col, inv_row, tm):
    rows = pos_col.shape[0]
    return pl.pallas_call(
        _rope_table_kernel,
        grid=(rows // tm,),
        in_specs=[pl.BlockSpec((tm, 1), lambda i: (i, 0)),
                  pl.BlockSpec((1, LANES), lambda i: (0, 0))],
        out_specs=[pl.BlockSpec((tm, LANES), lambda i: (i, 0))] * 2,
        out_shape=[jax.ShapeDtypeStruct((rows, LANES), F32)] * 2,
        compiler_params=_params("arbitrary"),
        name="rope_table",
    )(pos_col, inv_row)


def _ln_kernel(x_ref, g_ref, b_ref, o_ref):
    o_ref[...] = _layer_norm(x_ref[...], g_ref[...], b_ref[...])


def _input_ln(x2, g, b, tm):
    rows, d = x2.shape
    return pl.pallas_call(
        _ln_kernel,
        grid=(rows // tm,),
        in_specs=[pl.BlockSpec((tm, d), lambda i: (i, 0)),
                  pl.BlockSpec((1, d), lambda i: (0, 0)),
                  pl.BlockSpec((1, d), lambda i: (0, 0))],
        out_specs=pl.BlockSpec((tm, d), lambda i: (i, 0)),
        out_shape=jax.ShapeDtypeStruct((rows, d), F32),
        compiler_params=_params("arbitrary"),
        name="input_ln",
    )(x2, g, b)


def _inproj_kernel(x_ref, win_ref, wuq_ref, wukv_ref, qg_ref, kvg_ref, cos_ref, sin_ref,
                   q_ref, k_ref, v_ref, uc_ref, up_ref, *, heads, ql, kvl, cwid2, scale):
    dn, dv = QK_NOPE_DIM, V_HEAD_DIM
    o_kr = ql + kvl
    o_uc = o_kr + 2 * LANES
    o_up = o_uc + cwid2
    xb = x_ref[...].astype(BF16)
    cos = cos_ref[...]
    sin = sin_ref[...]

    uc_ref[...] = jnp.dot(xb, win_ref[:, o_uc:o_up], preferred_element_type=F32)
    up_ref[...] = jnp.dot(xb, win_ref[:, o_up:], preferred_element_type=F32)

    hk = jnp.dot(xb, win_ref[:, o_kr:o_uc], preferred_element_type=F32)
    kr = (hk[:, :LANES] * cos + hk[:, LANES:] * sin).astype(BF16)

    cq = jnp.dot(xb, win_ref[:, :ql], preferred_element_type=F32)
    cqn = _rms_norm(cq, qg_ref[...]).astype(BF16)
    q = jnp.dot(cqn, wuq_ref[...], preferred_element_type=F32)
    n_nope = heads * dn
    n_rope = heads * QK_ROPE_DIM
    lane = lax.broadcasted_iota(jnp.int32, (1, LANES), 1)
    for p in range(n_rope // LANES):
        a = q[:, n_nope + p * LANES:n_nope + (p + 1) * LANES]
        r = q[:, n_nope + n_rope + p * LANES:n_nope + n_rope + (p + 1) * LANES]
        roped = (a * cos + r * sin) * scale
        for s in range(2):
            h = 2 * p + s
            keep = (lane < QK_ROPE_DIM) if s == 0 else (lane >= QK_ROPE_DIM)
            q_ref[:, h * 2 * dn + dn:(h + 1) * 2 * dn] = jnp.where(keep, roped, 0.0).astype(BF16)
    for h in range(heads):
        q_ref[:, h * 2 * dn:h * 2 * dn + dn] = (q[:, h * dn:(h + 1) * dn] * scale).astype(BF16)

    ckv = jnp.dot(xb, win_ref[:, ql:o_kr], preferred_element_type=F32)
    ckvn = _rms_norm(ckv, kvg_ref[...]).astype(BF16)
    kv = jnp.dot(ckvn, wukv_ref[...], preferred_element_type=F32)
    for h in range(heads):
        k_ref[:, h * 2 * dn:h * 2 * dn + dn] = kv[:, h * dn:(h + 1) * dn].astype(BF16)
        k_ref[:, h * 2 * dn + dn:(h + 1) * 2 * dn] = kr
    v_ref[...] = kv[:, n_nope:].astype(BF16)


def _inproj(x2, win, wuq, wukv, qg, kvg, cos, sin, *, heads, cwid2, pwid, tm):
    rows, d = x2.shape
    ql = qg.shape[1]
    kvl = kvg.shape[1]
    dn, dv = QK_NOPE_DIM, V_HEAD_DIM
    scale = (QK_NOPE_DIM + QK_ROPE_DIM) ** -0.5
    row = lambda w: pl.BlockSpec((tm, w), lambda i: (i, 0))
    return pl.pallas_call(
        functools.partial(_inproj_kernel, heads=heads, ql=ql, kvl=kvl, cwid2=cwid2, scale=scale),
        grid=(rows // tm,),
        in_specs=[row(d), _resident(win.shape), _resident(wuq.shape), _resident(wukv.shape),
                  _resident(qg.shape), _resident(kvg.shape), row(LANES), row(LANES)],
        out_specs=[row(heads * 2 * dn), row(heads * 2 * dn), row(heads * dv), row(cwid2), row(pwid)],
        out_shape=[jax.ShapeDtypeStruct((rows, heads * 2 * dn), BF16),
                   jax.ShapeDtypeStruct((rows, heads * 2 * dn), BF16),
                   jax.ShapeDtypeStruct((rows, heads * dv), BF16),
                   jax.ShapeDtypeStruct((rows, cwid2), F32),
                   jax.ShapeDtypeStruct((rows, pwid), F32)],
        compiler_params=_params("arbitrary"),
        name="inproj",
    )(x2, win, wuq, wukv, qg, kvg, cos, sin)


def _attn_kernel(q_ref, k_ref, v_ref, o_ref, *, tq):
    seq = q_ref.shape[1]
    nt = (((1,), (1,)), ((), ()))
    row = lax.broadcasted_iota(jnp.int32, (tq, tq), 0)
    col = lax.broadcasted_iota(jnp.int32, (tq, tq), 1)
    neg = jnp.finfo(F32).min
    for iq in range(seq // tq):
        k0 = iq * tq
        q = q_ref[0, k0:k0 + tq, :]
        sd = lax.dot_general(q, k_ref[0, k0:k0 + tq, :], nt, preferred_element_type=F32)
        sd = jnp.where(col <= row, sd, neg)
        m = jnp.max(sd, axis=-1, keepdims=True)
        if k0 > 0:
            sp = lax.dot_general(q, k_ref[0, :k0, :], nt, preferred_element_type=F32)
            m = jnp.maximum(m, jnp.max(sp, axis=-1, keepdims=True))
            pp = jnp.exp(sp - m)
            l = jnp.sum(pp, axis=-1, keepdims=True)
            acc = jnp.dot(pp.astype(BF16), v_ref[0, :k0, :], preferred_element_type=F32)
        pd = jnp.exp(sd - m)
        ld = jnp.sum(pd, axis=-1, keepdims=True)
        accd = jnp.dot(pd.astype(BF16), v_ref[0, k0:k0 + tq, :], preferred_element_type=F32)
        if k0 > 0:
            l = l + ld
            acc = acc + accd
        else:
            l, acc = ld, accd
        o_ref[0, k0:k0 + tq, :] = (acc / l).astype(BF16)


def _attention(q3, k3, v3, *, heads, tq):
    bsz, seq, _ = q3.shape
    dqk = 2 * QK_NOPE_DIM
    dv = V_HEAD_DIM
    return pl.pallas_call(
        functools.partial(_attn_kernel, tq=tq),
        grid=(bsz, heads),
        in_specs=[pl.BlockSpec((1, seq, dqk), lambda b, h: (b, 0, h)),
                  pl.BlockSpec((1, seq, dqk), lambda b, h: (b, 0, h)),
                  pl.BlockSpec((1, seq, dv), lambda b, h: (b, 0, h))],
        out_specs=pl.BlockSpec((1, seq, dv), lambda b, h: (b, 0, h)),
        out_shape=jax.ShapeDtypeStruct((bsz, seq, heads * dv), BF16),
        compiler_params=_params("arbitrary", "arbitrary"),
        name="attention",
    )(q3, k3, v3)


CONV_HALO = 32
POOL_HALO = 16
CONV_CHUNK = 64


def _mixer_kernel(uc_ref, up_ref, cw_ref, cb_ref, lg_ref, lb_ref, wp_ref, ps_ref,
                  yc_ref, yp_ref, hbuf, hsh, pbuf, *, tm, cwid, taps):
    st = pl.program_id(1)

    @pl.when(st == 0)
    def _():
        hbuf[0:CONV_HALO, :] = jnp.zeros((CONV_HALO, cwid), F32)
        pbuf[0:POOL_HALO, :] = jnp.zeros((POOL_HALO, pbuf.shape[1]), F32)

    a = uc_ref[0, :, :cwid]
    g = uc_ref[0, :, cwid:]
    hbuf[CONV_HALO:CONV_HALO + tm, :] = a * _sigmoid(g)
    hsh[0] = hbuf[...]
    span = tm + CONV_HALO - SUBLANES
    for sh in range(1, SUBLANES):
        hsh[sh, 0:span, :] = hbuf[sh:sh + span, :]

    cb = cb_ref[...]
    lg = lg_ref[...]
    lb = lb_ref[...]
    base = CONV_HALO - (taps - 1)

    def chunk(c, carry):
        r0 = pl.multiple_of(c * CONV_CHUNK, CONV_CHUNK)
        acc = jnp.zeros((CONV_CHUNK, cwid), F32)
        for k in range(taps):
            off = base + k
            rows = pl.ds(r0 + (off // SUBLANES) * SUBLANES, CONV_CHUNK)
            acc = acc + cw_ref[k:k + 1, :] * hsh[off % SUBLANES, rows, :]
        y = _layer_norm(acc + cb, lg, lb)
        yc_ref[0, pl.ds(r0, CONV_CHUNK), :] = (y * _sigmoid(y)).astype(BF16)
        return carry

    lax.fori_loop(0, tm // CONV_CHUNK, chunk, 0)
    hbuf[0:CONV_HALO, :] = hbuf[tm:tm + CONV_HALO, :]

    pbuf[POOL_HALO:POOL_HALO + tm, :] = up_ref[0]
    t = st * tm + lax.broadcasted_iota(jnp.int32, (tm, 1), 0)
    for gi, w in enumerate(POOL_WINDOWS):
        lo, hi = gi * LANES, (gi + 1) * LANES
        ws = pbuf[POOL_HALO:POOL_HALO + tm, lo:hi]
        for i in range(1, w):
            ws = ws + pbuf[POOL_HALO - i:POOL_HALO - i + tm, lo:hi]
        cnt = jnp.minimum(t + 1, w).astype(F32)
        d = ws / cnt - pbuf[POOL_HALO:POOL_HALO + tm, lo:hi]
        y = jnp.dot(d.astype(BF16), wp_ref[gi], preferred_element_type=F32)
        yp_ref[0, :, lo:hi] = (y * ps_ref[:, lo:hi]).astype(BF16)
    pbuf[0:POOL_HALO, :] = pbuf[tm:tm + POOL_HALO, :]


def _mixers(uc3, up3, cw, cb, lg, lb, wp, ps, *, tm):
    bsz, seq, cwid2 = uc3.shape
    cwid = cwid2 // 2
    pwid = up3.shape[2]
    taps = cw.shape[0]
    assert taps - 1 <= CONV_HALO and max(POOL_WINDOWS) - 1 <= POOL_HALO
    assert pwid == len(POOL_WINDOWS) * LANES and tm % CONV_CHUNK == 0
    return pl.pallas_call(
        functools.partial(_mixer_kernel, tm=tm, cwid=cwid, taps=taps),
        grid=(bsz, seq // tm),
        in_specs=[pl.BlockSpec((1, tm, cwid2), lambda b, s: (b, s, 0)),
                  pl.BlockSpec((1, tm, pwid), lambda b, s: (b, s, 0)),
                  _resident(cw.shape), _resident(cb.shape), _resident(lg.shape), _resident(lb.shape),
                  _resident(wp.shape), _resident(ps.shape)],
        out_specs=[pl.BlockSpec((1, tm, cwid), lambda b, s: (b, s, 0)),
                   pl.BlockSpec((1, tm, pwid), lambda b, s: (b, s, 0))],
        out_shape=[jax.ShapeDtypeStruct((bsz, seq, cwid), BF16),
                   jax.ShapeDtypeStruct((bsz, seq, pwid), BF16)],
        scratch_shapes=[pltpu.VMEM((tm + CONV_HALO, cwid), F32),
                        pltpu.VMEM((SUBLANES, tm + CONV_HALO, cwid), F32),
                        pltpu.VMEM((tm + POOL_HALO, pwid), F32)],
        compiler_params=_params("arbitrary", "arbitrary"),
        name="mixers",
    )(uc3, up3, cw, cb, lg, lb, wp, ps)


def _outproj_kernel(ym_ref, yc_ref, yp_ref, x_ref, w_ref, g_ref, b_ref, o_ref, *, alpha):
    mixed = jnp.concatenate([ym_ref[...], yc_ref[...], yp_ref[...]], axis=1)
    y = jnp.dot(mixed, w_ref[...], preferred_element_type=F32)
    o_ref[...] = _layer_norm(alpha * x_ref[...] + y, g_ref[...], b_ref[...])


def _outproj(ym, yc, yp, x2, w, g, b, *, alpha, tm):
    rows, d = x2.shape
    row = lambda wd: pl.BlockSpec((tm, wd), lambda i: (i, 0))
    return pl.pallas_call(
        functools.partial(_outproj_kernel, alpha=alpha),
        grid=(rows // tm,),
        in_specs=[row(ym.shape[1]), row(yc.shape[1]), row(yp.shape[1]), row(d),
                  _resident(w.shape), _resident(g.shape), _resident(b.shape)],
        out_specs=row(d),
        out_shape=jax.ShapeDtypeStruct((rows, d), F32),
        compiler_params=_params("arbitrary"),
        name="outproj",
    )(ym, yc, yp, x2, w, g, b)


FFN_HALO = 8
FFN_CHUNK = 64
FFN_HALVES = 2


def _ffn_kernel(x_ref, wa_ref, wg_ref, cwa_ref, cwg_ref, cba_ref, cbg_ref, wd_ref, g_ref, b_ref,
                o_ref, xb_ref, ua_buf, ug_buf, hid0, hid1, halo_a, halo_g, *, tm, tf, tiles_per_seq, alpha):
    i = pl.program_id(0)
    j = pl.program_id(1)
    nj = pl.num_programs(1)

    th = tf // FFN_HALVES

    @pl.when(j == 0)
    def _():
        x = x_ref[...]
        xb_ref[...] = x.astype(BF16)
        o_ref[...] = alpha * x

    @pl.when(jnp.logical_and(i == 0, j == 0))
    def _():
        halo_a[...] = jnp.zeros(halo_a.shape, F32)
        halo_g[...] = jnp.zeros(halo_g.shape, F32)

    keep = (i % tiles_per_seq) != 0
    ua_buf[0:FFN_HALO, :] = jnp.where(keep, halo_a[j], 0.0)
    ug_buf[0:FFN_HALO, :] = jnp.where(keep, halo_g[j], 0.0)

    xb = xb_ref[...]
    hids = (hid0, hid1)
    for h in range(FFN_HALVES):
        cols = slice(h * th, (h + 1) * th)
        ua_buf[FFN_HALO:FFN_HALO + tm, cols] = jnp.dot(xb, wa_ref[:, cols], preferred_element_type=F32)
        ug_buf[FFN_HALO:FFN_HALO + tm, cols] = jnp.dot(xb, wg_ref[:, cols], preferred_element_type=F32)
        cwa, cwg = cwa_ref[:, cols], cwg_ref[:, cols]
        cba, cbg = cba_ref[:, cols], cbg_ref[:, cols]
        for c in range(tm // FFN_CHUNK):
            r = FFN_HALO + c * FFN_CHUNK

            def conv(buf, cw, cb):
                return (cw[2:3, :] * buf[r:r + FFN_CHUNK, cols]
                        + cw[1:2, :] * buf[r - 1:r - 1 + FFN_CHUNK, cols]
                        + cw[0:1, :] * buf[r - 2:r - 2 + FFN_CHUNK, cols] + cb)

            ca = conv(ua_buf, cwa, cba)
            cg = conv(ug_buf, cwg, cbg)
            hids[h][c * FFN_CHUNK:(c + 1) * FFN_CHUNK, :] = (ca * (cg * _sigmoid(cg))).astype(BF16)
    halo_a[j] = ua_buf[tm:tm + FFN_HALO, :]
    halo_g[j] = ug_buf[tm:tm + FFN_HALO, :]
    for h in range(FFN_HALVES):
        o_ref[...] += jnp.dot(hids[h][...], wd_ref[h * th:(h + 1) * th, :], preferred_element_type=F32)

    @pl.when(j == nj - 1)
    def _():
        o_ref[...] = _layer_norm(o_ref[...], g_ref[...], b_ref[...])


def _ffn(x2, wup, cw, cb, wdn, g, b, *, seq, alpha, tm, tf):
    rows, d = x2.shape
    dff = wdn.shape[0]
    nj = dff // tf
    th = tf // FFN_HALVES
    assert seq % tm == 0 and dff % tf == 0 and cw.shape[0] - 1 <= FFN_HALO
    assert FFN_HALVES == 2 and th % LANES == 0 and tm % FFN_CHUNK == 0
    return pl.pallas_call(
        functools.partial(_ffn_kernel, tm=tm, tf=tf, tiles_per_seq=seq // tm, alpha=alpha),
        grid=(rows // tm, nj),
        in_specs=[pl.BlockSpec((tm, d), lambda i, j: (i, 0)),
                  pl.BlockSpec((d, tf), lambda i, j: (0, j)),
                  pl.BlockSpec((d, tf), lambda i, j: (0, j + nj)),
                  pl.BlockSpec((cw.shape[0], tf), lambda i, j: (0, j)),
                  pl.BlockSpec((cw.shape[0], tf), lambda i, j: (0, j + nj)),
                  pl.BlockSpec((1, tf), lambda i, j: (0, j)),
                  pl.BlockSpec((1, tf), lambda i, j: (0, j + nj)),
                  pl.BlockSpec((tf, d), lambda i, j: (j, 0)),
                  _resident(g.shape), _resident(b.shape)],
        out_specs=pl.BlockSpec((tm, d), lambda i, j: (i, 0)),
        out_shape=jax.ShapeDtypeStruct((rows, d), F32),
        scratch_shapes=[pltpu.VMEM((tm, d), BF16),
                        pltpu.VMEM((tm + FFN_HALO, tf), F32),
                        pltpu.VMEM((tm + FFN_HALO, tf), F32),
                        pltpu.VMEM((tm, th), BF16),
                        pltpu.VMEM((tm, th), BF16),
                        pltpu.VMEM((nj, FFN_HALO, tf), F32),
                        pltpu.VMEM((nj, FFN_HALO, tf), F32)],
        compiler_params=_params("arbitrary", "arbitrary"),
        name="ffn",
    )(x2, wup, wup, cw, cw, cb, cb, wdn, g, b)


def _rot_half(w):
    half = w.shape[-1] // 2
    return jnp.concatenate([-w[..., half:], w[..., :half]], axis=-1)


def _prep_in_weights(w_in_l, ql, kvl, cwid2):
    o_kr = ql + kvl
    o_uc = o_kr + QK_ROPE_DIM
    kr = w_in_l[:, o_kr:o_uc]
    krr = _rot_half(kr)
    return jnp.concatenate([w_in_l[:, :o_kr], kr, kr, krr, krr, w_in_l[:, o_uc:]], axis=1).astype(BF16)


def _prep_uq(w_uq_l):
    r, heads, _ = w_uq_l.shape
    nope = w_uq_l[:, :, :QK_NOPE_DIM].reshape(r, heads * QK_NOPE_DIM)
    rope = w_uq_l[:, :, QK_NOPE_DIM:]
    return jnp.concatenate([nope, rope.reshape(r, -1), _rot_half(rope).reshape(r, -1)], axis=1).astype(BF16)


def _prep_ukv(w_ukv_l):
    r, heads, _ = w_ukv_l.shape
    kn = w_ukv_l[:, :, :QK_NOPE_DIM].reshape(r, heads * QK_NOPE_DIM)
    v = w_ukv_l[:, :, QK_NOPE_DIM:].reshape(r, heads * V_HEAD_DIM)
    return jnp.concatenate([kn, v], axis=1).astype(BF16)


def _pick(n, pref):
    t = min(n, pref)
    assert n % t == 0
    return t


def kernel(x, positions, ln_in_g, ln_in_b, w_in, q_norm_g, w_uq, kv_norm_g, w_ukv, conv_w, conv_b, conv_ln_g, conv_ln_b, w_pool, pool_scale, w_out, ln1_g, ln1_b, w_up, ffn_conv_w, ffn_conv_b, w_down, ln2_g, ln2_b):
    bsz, seq, d = x.shape
    depth = w_in.shape[0]
    rows = bsz * seq
    heads = w_uq.shape[2]
    ql = q_norm_g.shape[1]
    kvl = kv_norm_g.shape[1]
    cwid = conv_w.shape[2]
    pwid = pool_scale.shape[1]
    alpha = (2.0 * depth) ** 0.25

    inv = 1.0 / (ROPE_THETA ** (jnp.arange(0, QK_ROPE_DIM, 2, dtype=F32) / QK_ROPE_DIM))
    inv_row = jnp.tile(inv, LANES // inv.shape[0])[None, :]
    cos, sin = _rope_table(positions.reshape(rows, 1), inv_row, _pick(rows, 1024))

    xs = _input_ln(x.reshape(rows, d), ln_in_g[None, :], ln_in_b[None, :], _pick(rows, 512))

    for l in range(depth):
        q2, k2, v2, uc, up = _inproj(
            xs, _prep_in_weights(w_in[l], ql, kvl, 2 * cwid), _prep_uq(w_uq[l]), _prep_ukv(w_ukv[l]),
            q_norm_g[l][None, :], kv_norm_g[l][None, :], cos, sin,
            heads=heads, cwid2=2 * cwid, pwid=pwid, tm=_pick(rows, 256))
        ym = _attention(q2.reshape(bsz, seq, -1), k2.reshape(bsz, seq, -1), v2.reshape(bsz, seq, -1),
                        heads=heads, tq=_pick(seq, 256))
        yc, yp = _mixers(uc.reshape(bsz, seq, -1), up.reshape(bsz, seq, -1),
                         conv_w[l], conv_b[l][None, :], conv_ln_g[l][None, :], conv_ln_b[l][None, :],
                         w_pool[l].astype(BF16), pool_scale[l][None, :], tm=_pick(seq, 512))
        xs = _outproj(ym.reshape(rows, -1), yc.reshape(rows, -1), yp.reshape(rows, -1), xs,
                      w_out[l].astype(BF16), ln1_g[l][None, :], ln1_b[l][None, :],
                      alpha=alpha, tm=_pick(rows, 512))
        xs = _ffn(xs, w_up[l].astype(BF16), ffn_conv_w[l], ffn_conv_b[l][None, :], w_down[l].astype(BF16),
                  ln2_g[l][None, :], ln2_b[l][None, :],
                  seq=seq, alpha=alpha, tm=_pick(seq, 512), tf=_pick(w_down.shape[1], 512))
    return xs.reshape(bsz, seq, d)
```

```python
import functools

import jax
import jax.numpy as jnp
from jax import lax
from jax.experimental import pallas as pl
from jax.experimental.pallas import tpu as pltpu

F32 = jnp.float32
BF16 = jnp.bfloat16

QK_NOPE_DIM = 128
QK_ROPE_DIM = 64
V_HEAD_DIM = 128
POOL_WINDOWS = (2, 4, 8, 16)
ROPE_THETA = 10000.0
LN_EPS = 1e-5
RMS_EPS = 1e-6
LOG2_E = 1.4426950408889634

LANES = 128
SUBLANES = 8
VMEM_LIMIT_BYTES = 56 * 1024 * 1024


def _params(*sem):
    return pltpu.CompilerParams(dimension_semantics=sem, vmem_limit_bytes=VMEM_LIMIT_BYTES)


def _resident(shape):
    return pl.BlockSpec(shape, lambda *_: (0,) * len(shape), pipeline_mode=pl.Buffered(1))


def _layer_norm(z, g, b):
    mu = jnp.mean(z, axis=-1, keepdims=True)
    zc = z - mu
    var = jnp.mean(zc * zc, axis=-1, keepdims=True)
    return zc * lax.rsqrt(var + LN_EPS) * g + b


def _rms_norm(z, g):
    ms = jnp.mean(z * z, axis=-1, keepdims=True)
    return z * lax.rsqrt(ms + RMS_EPS) * g


def _sigmoid(z):
    return 1.0 / (1.0 + jnp.exp(-z))


def _rope_table_kernel(pos_ref, inv_ref, cos_ref, sin_ref):
    ang = pos_ref[...].astype(F32) * inv_ref[...]
    cos_ref[...] = jnp.cos(ang)
    sin_ref[...] = jnp.sin(ang)


def _rope_table(pos_col, inv_row, tm):
    rows = pos_col.shape[0]
    return pl.pallas_call(
        _rope_table_kernel,
        grid=(rows // tm,),
        in_specs=[pl.BlockSpec((tm, 1), lambda i: (i, 0)),
                  pl.BlockSpec((1, LANES), lambda i: (0, 0))],
        out_specs=[pl.BlockSpec((tm, LANES), lambda i: (i, 0))] * 2,
        out_shape=[jax.ShapeDtypeStruct((rows, LANES), F32)] * 2,
        compiler_params=_params("arbitrary"),
        name="rope_table",
    )(pos_col, inv_row)


def _ln_kernel(x_ref, g_ref, b_ref, o_ref):
    o_ref[...] = _layer_norm(x_ref[...], g_ref[...], b_ref[...])


def _input_ln(x2, g, b, tm):
    rows, d = x2.shape
    return pl.pallas_call(
        _ln_kernel,
        grid=(rows // tm,),
        in_specs=[pl.BlockSpec((tm, d), lambda i: (i, 0)),
                  pl.BlockSpec((1, d), lambda i: (0, 0)),
                  pl.BlockSpec((1, d), lambda i: (0, 0))],
        out_specs=pl.BlockSpec((tm, d), lambda i: (i, 0)),
        out_shape=jax.ShapeDtypeStruct((rows, d), F32),
        compiler_params=_params("arbitrary"),
        name="input_ln",
    )(x2, g, b)


def _inproj_kernel(x_ref, win_ref, wuq_ref, wukv_ref, qg_ref, kvg_ref, cos_ref, sin_ref,
                   q_ref, k_ref, v_ref, uc_ref, up_ref, *, heads, ql, kvl, cwid2, scale):
    dn, dv = QK_NOPE_DIM, V_HEAD_DIM
    o_kr = ql + kvl
    o_uc = o_kr + 2 * LANES
    o_up = o_uc + cwid2
    xb = x_ref[...].astype(BF16)
    cos = cos_ref[...]
    sin = sin_ref[...]

    cq = jnp.dot(xb, win_ref[:, :ql], preferred_element_type=F32)
    ckv = jnp.dot(xb, win_ref[:, ql:o_kr], preferred_element_type=F32)
    hk = jnp.dot(xb, win_ref[:, o_kr:o_uc], preferred_element_type=F32)
    cqn = _rms_norm(cq, qg_ref[...]).astype(BF16)
    ckvn = _rms_norm(ckv, kvg_ref[...]).astype(BF16)
    kr = (hk[:, :LANES] * cos + hk[:, LANES:] * sin).astype(BF16)

    uc_ref[...] = jnp.dot(xb, win_ref[:, o_uc:o_up], preferred_element_type=F32)
    up_ref[...] = jnp.dot(xb, win_ref[:, o_up:], preferred_element_type=F32)

    q = jnp.dot(cqn, wuq_ref[...], preferred_element_type=F32)
    n_nope = heads * dn
    n_rope = heads * QK_ROPE_DIM
    lane = lax.broadcasted_iota(jnp.int32, (1, LANES), 1)
    for p in range(n_rope // LANES):
        a = q[:, n_nope + p * LANES:n_nope + (p + 1) * LANES]
        r = q[:, n_nope + n_rope + p * LANES:n_nope + n_rope + (p + 1) * LANES]
        roped = (a * cos + r * sin) * scale
        for s in range(2):
            h = 2 * p + s
            keep = (lane < QK_ROPE_DIM) if s == 0 else (lane >= QK_ROPE_DIM)
            q_ref[:, h * 2 * dn + dn:(h + 1) * 2 * dn] = jnp.where(keep, roped, 0.0).astype(BF16)
    for h in range(heads):
        q_ref[:, h * 2 * dn:h * 2 * dn + dn] = (q[:, h * dn:(h + 1) * dn] * scale).astype(BF16)

    kv = jnp.dot(ckvn, wukv_ref[...], preferred_element_type=F32)
    for h in range(heads):
        k_ref[:, h * 2 * dn:h * 2 * dn + dn] = kv[:, h * dn:(h + 1) * dn].astype(BF16)
        k_ref[:, h * 2 * dn + dn:(h + 1) * 2 * dn] = kr
    v_ref[...] = kv[:, n_nope:].astype(BF16)


def _inproj(x2, win, wuq, wukv, qg, kvg, cos, sin, *, heads, cwid2, pwid, tm):
    rows, d = x2.shape
    ql = qg.shape[1]
    kvl = kvg.shape[1]
    dn, dv = QK_NOPE_DIM, V_HEAD_DIM
    scale = (QK_NOPE_DIM + QK_ROPE_DIM) ** -0.5 * LOG2_E
    row = lambda w: pl.BlockSpec((tm, w), lambda i: (i, 0))
    return pl.pallas_call(
        functools.partial(_inproj_kernel, heads=heads, ql=ql, kvl=kvl, cwid2=cwid2, scale=scale),
        grid=(rows // tm,),
        in_specs=[row(d), _resident(win.shape), _resident(wuq.shape), _resident(wukv.shape),
                  _resident(qg.shape), _resident(kvg.shape), row(LANES), row(LANES)],
        out_specs=[row(heads * 2 * dn), row(heads * 2 * dn), row(heads * dv), row(cwid2), row(pwid)],
        out_shape=[jax.ShapeDtypeStruct((rows, heads * 2 * dn), BF16),
                   jax.ShapeDtypeStruct((rows, heads * 2 * dn), BF16),
                   jax.ShapeDtypeStruct((rows, heads * dv), BF16),
                   jax.ShapeDtypeStruct((rows, cwid2), F32),
                   jax.ShapeDtypeStruct((rows, pwid), F32)],
        compiler_params=_params("arbitrary"),
        name="inproj",
    )(x2, win, wuq, wukv, qg, kvg, cos, sin)


ATTN_PAD_ROWS = 16


def _attn_kernel(q_ref, k_ref, v_ref, o_ref, vt_ref, m_ref, acc_ref, *, tq):
    seq = q_ref.shape[1]
    nb = seq // tq
    nt = (((1,), (1,)), ((), ()))
    dv = v_ref.shape[2]
    vt_ref[0:dv, :] = v_ref[0].astype(F32).T.astype(BF16)
    ones_row = lax.broadcasted_iota(jnp.int32, (ATTN_PAD_ROWS, seq), 0) == 0
    vt_ref[dv:, :] = jnp.where(ones_row, 1.0, 0.0).astype(BF16)
    key = lax.broadcasted_iota(jnp.int32, (tq, tq), 0)
    qry = lax.broadcasted_iota(jnp.int32, (tq, tq), 1)
    neg = jnp.finfo(F32).min
    for j in range(nb):
        kj = k_ref[0, j * tq:(j + 1) * tq, :]
        s_all = lax.dot_general(kj, q_ref[0, j * tq:, :], nt, preferred_element_type=F32)
        vtj = vt_ref[:, j * tq:(j + 1) * tq]
        for i in range(j, nb):
            cols = slice(i * tq, (i + 1) * tq)
            s = s_all[:, (i - j) * tq:(i - j + 1) * tq]
            if i == j:
                s = jnp.where(key <= qry, s, neg)
            m_new = jnp.max(s, axis=0, keepdims=True)
            if j > 0:
                m_old = m_ref[:, cols]
                m_new = jnp.maximum(m_old, m_new)
            p = jnp.exp2(s - m_new)
            acc = jnp.dot(vtj, p.astype(BF16), preferred_element_type=F32)
            if j > 0:
                acc = jnp.exp2(m_old - m_new) * acc_ref[:, cols] + acc
            if i > j:
                m_ref[:, cols] = m_new
                acc_ref[:, cols] = acc
            else:
                o_ref[0, cols, :] = (acc[0:dv, :] / acc[dv:dv + 1, :]).T.astype(BF16)


def _attention(q3, k3, v3, *, heads, tq):
    bsz, seq, _ = q3.shape
    dqk = 2 * QK_NOPE_DIM
    dv = V_HEAD_DIM
    return pl.pallas_call(
        functools.partial(_attn_kernel, tq=tq),
        grid=(bsz, heads),
        in_specs=[pl.BlockSpec((1, seq, dqk), lambda b, h: (b, 0, h)),
                  pl.BlockSpec((1, seq, dqk), lambda b, h: (b, 0, h)),
                  pl.BlockSpec((1, seq, dv), lambda b, h: (b, 0, h))],
        out_specs=pl.BlockSpec((1, seq, dv), lambda b, h: (b, 0, h)),
        out_shape=jax.ShapeDtypeStruct((bsz, seq, heads * dv), BF16),
        scratch_shapes=[pltpu.VMEM((dv + ATTN_PAD_ROWS, seq), BF16),
                        pltpu.VMEM((1, seq), F32),
                        pltpu.VMEM((dv + ATTN_PAD_ROWS, seq), F32)],
        compiler_params=_params("arbitrary", "arbitrary"),
        name="attention",
    )(q3, k3, v3)


CONV_HALO = 32
POOL_HALO = 16
CONV_CHUNK = 32


def _mixer_kernel(uc_ref, up_ref, cw_ref, cb_ref, lg_ref, lb_ref, wp_ref, ps_ref,
                  yc_ref, yp_ref, hbuf, hsh, ybuf, wrep, pbuf, *, tm, cwid, taps):
    st = pl.program_id(1)

    @pl.when(st == 0)
    def _():
        hbuf[0:CONV_HALO, :] = jnp.zeros((CONV_HALO, cwid), F32)
        pbuf[0:POOL_HALO, :] = jnp.zeros((POOL_HALO, pbuf.shape[1]), F32)

    a = uc_ref[0, :, :cwid]
    g = uc_ref[0, :, cwid:]
    hbuf[CONV_HALO:CONV_HALO + tm, :] = a * _sigmoid(g)
    span = tm + CONV_HALO - SUBLANES
    for sh in range(1, SUBLANES):
        hsh[sh - 1, 0:span, :] = hbuf[sh:sh + span, :]

    cb = cb_ref[...]
    lg = lg_ref[...]
    lb = lb_ref[...]
    base = CONV_HALO - (taps - 1)

    @pl.when(jnp.logical_and(pl.program_id(0) == 0, st == 0))
    def _():
        for k in range(taps):
            wrep[k] = jnp.broadcast_to(cw_ref[k:k + 1, :], (SUBLANES, cwid))

    nsub = CONV_CHUNK // SUBLANES

    def chunk(c, carry):
        r0 = pl.multiple_of(c * CONV_CHUNK, CONV_CHUNK)
        acc = [jnp.zeros((SUBLANES, cwid), F32) for _ in range(nsub)]
        for k in range(taps):
            off = base + k
            sh = off % SUBLANES
            w8 = wrep[k]
            for s in range(nsub):
                rows = pl.ds(r0 + (off // SUBLANES + s) * SUBLANES, SUBLANES)
                acc[s] = acc[s] + w8 * (hbuf[rows, :] if sh == 0 else hsh[sh - 1, rows, :])
        for s in range(nsub):
            ybuf[pl.ds(r0 + s * SUBLANES, SUBLANES), :] = acc[s]
        return carry

    lax.fori_loop(0, tm // CONV_CHUNK, chunk, 0)
    hbuf[0:CONV_HALO, :] = hbuf[tm:tm + CONV_HALO, :]
    y = _layer_norm(ybuf[...] + cb, lg, lb)
    yc_ref[0] = (y * _sigmoid(y)).astype(BF16)

    pbuf[POOL_HALO:POOL_HALO + tm, :] = up_ref[0]
    t = st * tm + lax.broadcasted_iota(jnp.int32, (tm, 1), 0)
    for gi, w in enumerate(POOL_WINDOWS):
        lo, hi = gi * LANES, (gi + 1) * LANES
        ws = pbuf[POOL_HALO:POOL_HALO + tm, lo:hi]
        for i in range(1, w):
            ws = ws + pbuf[POOL_HALO - i:POOL_HALO - i + tm, lo:hi]
        cnt = jnp.minimum(t + 1, w).astype(F32)
        d = ws / cnt - pbuf[POOL_HALO:POOL_HALO + tm, lo:hi]
        y = jnp.dot(d.astype(BF16), wp_ref[gi], preferred_element_type=F32)
        yp_ref[0, :, lo:hi] = (y * ps_ref[:, lo:hi]).astype(BF16)
    pbuf[0:POOL_HALO, :] = pbuf[tm:tm + POOL_HALO, :]


def _mixers(uc3, up3, cw, cb, lg, lb, wp, ps, *, tm):
    bsz, seq, cwid2 = uc3.shape
    cwid = cwid2 // 2
    pwid = up3.shape[2]
    taps = cw.shape[0]
    assert taps - 1 <= CONV_HALO and max(POOL_WINDOWS) - 1 <= POOL_HALO
    assert pwid == len(POOL_WINDOWS) * LANES and tm % CONV_CHUNK == 0
    return pl.pallas_call(
        functools.partial(_mixer_kernel, tm=tm, cwid=cwid, taps=taps),
        grid=(bsz, seq // tm),
        in_specs=[pl.BlockSpec((1, tm, cwid2), lambda b, s: (b, s, 0)),
                  pl.BlockSpec((1, tm, pwid), lambda b, s: (b, s, 0)),
                  _resident(cw.shape), _resident(cb.shape), _resident(lg.shape), _resident(lb.shape),
                  _resident(wp.shape), _resident(ps.shape)],
        out_specs=[pl.BlockSpec((1, tm, cwid), lambda b, s: (b, s, 0)),
                   pl.BlockSpec((1, tm, pwid), lambda b, s: (b, s, 0))],
        out_shape=[jax.ShapeDtypeStruct((bsz, seq, cwid), BF16),
                   jax.ShapeDtypeStruct((bsz, seq, pwid), BF16)],
        scratch_shapes=[pltpu.VMEM((tm + CONV_HALO, cwid), F32),
                        pltpu.VMEM((SUBLANES - 1, tm + CONV_HALO, cwid), F32),
                        pltpu.VMEM((tm, cwid), F32),
                        pltpu.VMEM((taps, SUBLANES, cwid), F32),
                        pltpu.VMEM((tm + POOL_HALO, pwid), F32)],
        compiler_params=_params("arbitrary", "arbitrary"),
        name="mixers",
    )(uc3, up3, cw, cb, lg, lb, wp, ps)


def _outproj_kernel(ym_ref, yc_ref, yp_ref, x_ref, w_ref, g_ref, b_ref, o_ref, *, alpha):
    mixed = jnp.concatenate([ym_ref[...], yc_ref[...], yp_ref[...]], axis=1)
    y = jnp.dot(mixed, w_ref[...], preferred_element_type=F32)
    o_ref[...] = _layer_norm(alpha * x_ref[...] + y, g_ref[...], b_ref[...])


def _outproj(ym, yc, yp, x2, w, g, b, *, alpha, tm):
    rows, d = x2.shape
    row = lambda wd: pl.BlockSpec((tm, wd), lambda i: (i, 0))
    return pl.pallas_call(
        functools.partial(_outproj_kernel, alpha=alpha),
        grid=(rows // tm,),
        in_specs=[row(ym.shape[1]), row(yc.shape[1]), row(yp.shape[1]), row(d),
                  _resident(w.shape), _resident(g.shape), _resident(b.shape)],
        out_specs=row(d),
        out_shape=jax.ShapeDtypeStruct((rows, d), F32),
        compiler_params=_params("arbitrary"),
        name="outproj",
    )(ym, yc, yp, x2, w, g, b)


FFN_HALO = 8
FFN_CHUNK = 64
FFN_HALVES = 2


def _ffn_kernel(x_ref, wa_ref, wg_ref, cwa_ref, cwg_ref, cba_ref, cbg_ref, wd_ref, g_ref, b_ref,
                o_ref, xb_ref, ua_buf, ug_buf, hid0, hid1, halo_a, halo_g, *, tm, tf, tiles_per_seq, alpha):
    i = pl.program_id(0)
    j = pl.program_id(1)
    nj = pl.num_programs(1)

    th = tf // FFN_HALVES

    @pl.when(j == 0)
    def _():
        x = x_ref[...]
        xb_ref[...] = x.astype(BF16)
        o_ref[...] = alpha * x

    @pl.when(jnp.logical_and(i == 0, j == 0))
    def _():
        halo_a[...] = jnp.zeros(halo_a.shape, F32)
        halo_g[...] = jnp.zeros(halo_g.shape, F32)

    keep = (i % tiles_per_seq) != 0
    ua_buf[0:FFN_HALO, :] = jnp.where(keep, halo_a[j], 0.0)
    ug_buf[0:FFN_HALO, :] = jnp.where(keep, halo_g[j], 0.0)

    xb = xb_ref[...]
    hids = (hid0, hid1)
    for h in range(FFN_HALVES):
        cols = slice(h * th, (h + 1) * th)
        ua_buf[FFN_HALO:FFN_HALO + tm, cols] = jnp.dot(xb, wa_ref[:, cols], preferred_element_type=F32)
        ug_buf[FFN_HALO:FFN_HALO + tm, cols] = jnp.dot(xb, wg_ref[:, cols], preferred_element_type=F32)
        cwa, cwg = cwa_ref[:, cols], cwg_ref[:, cols]
        cba, cbg = cba_ref[:, cols], cbg_ref[:, cols]
        for c in range(tm // FFN_CHUNK):
            r = FFN_HALO + c * FFN_CHUNK

            def conv(buf, cw, cb):
                return (cw[2:3, :] * buf[r:r + FFN_CHUNK, cols]
                        + cw[1:2, :] * buf[r - 1:r - 1 + FFN_CHUNK, cols]
                        + cw[0:1, :] * buf[r - 2:r - 2 + FFN_CHUNK, cols] + cb)

            ca = conv(ua_buf, cwa, cba)
            cg = conv(ug_buf, cwg, cbg)
            hids[h][c * FFN_CHUNK:(c + 1) * FFN_CHUNK, :] = (ca * (cg * _sigmoid(cg))).astype(BF16)
    halo_a[j] = ua_buf[tm:tm + FFN_HALO, :]
    halo_g[j] = ug_buf[tm:tm + FFN_HALO, :]
    for h in range(FFN_HALVES):
        o_ref[...] += jnp.dot(hids[h][...], wd_ref[h * th:(h + 1) * th, :], preferred_element_type=F32)

    @pl.when(j == nj - 1)
    def _():
        o_ref[...] = _layer_norm(o_ref[...], g_ref[...], b_ref[...])


def _ffn(x2, wup, cw, cb, wdn, g, b, *, seq, alpha, tm, tf):
    rows, d = x2.shape
    dff = wdn.shape[0]
    nj = dff // tf
    th = tf // FFN_HALVES
    assert seq % tm == 0 and dff % tf == 0 and cw.shape[0] - 1 <= FFN_HALO
    assert FFN_HALVES == 2 and th % LANES == 0 and tm % FFN_CHUNK == 0
    return pl.pallas_call(
        functools.partial(_ffn_kernel, tm=tm, tf=tf, tiles_per_seq=seq // tm, alpha=alpha),
        grid=(rows // tm, nj),
        in_specs=[pl.BlockSpec((tm, d), lambda i, j: (i, 0)),
                  pl.BlockSpec((d, tf), lambda i, j: (0, j)),
                  pl.BlockSpec((d, tf), lambda i, j: (0, j + nj)),
                  pl.BlockSpec((cw.shape[0], tf), lambda i, j: (0, j)),
                  pl.BlockSpec((cw.shape[0], tf), lambda i, j: (0, j + nj)),
                  pl.BlockSpec((1, tf), lambda i, j: (0, j)),
                  pl.BlockSpec((1, tf), lambda i, j: (0, j + nj)),
                  pl.BlockSpec((tf, d), lambda i, j: (j, 0)),
                  _resident(g.shape), _resident(b.shape)],
        out_specs=pl.BlockSpec((tm, d), lambda i, j: (i, 0)),
        out_shape=jax.ShapeDtypeStruct((rows, d), F32),
        scratch_shapes=[pltpu.VMEM((tm, d), BF16),
                        pltpu.VMEM((tm + FFN_HALO, tf), F32),
                        pltpu.VMEM((tm + FFN_HALO, tf), F32),
                        pltpu.VMEM((tm, th), BF16),
                        pltpu.VMEM((tm, th), BF16),
                        pltpu.VMEM((nj, FFN_HALO, tf), F32),
                        pltpu.VMEM((nj, FFN_HALO, tf), F32)],
        compiler_params=_params("arbitrary", "arbitrary"),
        name="ffn",
    )(x2, wup, wup, cw, cw, cb, cb, wdn, g, b)


def _rot_half(w):
    half = w.shape[-1] // 2
    return jnp.concatenate([-w[..., half:], w[..., :half]], axis=-1)


def _prep_in_weights(w_in_l, ql, kvl, cwid2):
    o_kr = ql + kvl
    o_uc = o_kr + QK_ROPE_DIM
    kr = w_in_l[:, o_kr:o_uc]
    krr = _rot_half(kr)
    return jnp.concatenate([w_in_l[:, :o_kr], kr, kr, krr, krr, w_in_l[:, o_uc:]], axis=1).astype(BF16)


def _prep_uq(w_uq_l):
    r, heads, _ = w_uq_l.shape
    nope = w_uq_l[:, :, :QK_NOPE_DIM].reshape(r, heads * QK_NOPE_DIM)
    rope = w_uq_l[:, :, QK_NOPE_DIM:]
    return jnp.concatenate([nope, rope.reshape(r, -1), _rot_half(rope).reshape(r, -1)], axis=1).astype(BF16)


def _prep_ukv(w_ukv_l):
    r, heads, _ = w_ukv_l.shape
    kn = w_ukv_l[:, :, :QK_NOPE_DIM].reshape(r, heads * QK_NOPE_DIM)
    v = w_ukv_l[:, :, QK_NOPE_DIM:].reshape(r, heads * V_HEAD_DIM)
    return jnp.concatenate([kn, v], axis=1).astype(BF16)


def _pick(n, pref):
    t = min(n, pref)
    assert n % t == 0
    return t


def kernel(x, positions, ln_in_g, ln_in_b, w_in, q_norm_g, w_uq, kv_norm_g, w_ukv, conv_w, conv_b, conv_ln_g, conv_ln_b, w_pool, pool_scale, w_out, ln1_g, ln1_b, w_up, ffn_conv_w, ffn_conv_b, w_down, ln2_g, ln2_b):
    bsz, seq, d = x.shape
    depth = w_in.shape[0]
    rows = bsz * seq
    heads = w_uq.shape[2]
    ql = q_norm_g.shape[1]
    kvl = kv_norm_g.shape[1]
    cwid = conv_w.shape[2]
    pwid = pool_scale.shape[1]
    alpha = (2.0 * depth) ** 0.25

    inv = 1.0 / (ROPE_THETA ** (jnp.arange(0, QK_ROPE_DIM, 2, dtype=F32) / QK_ROPE_DIM))
    inv_row = jnp.tile(inv, LANES // inv.shape[0])[None, :]
    cos, sin = _rope_table(positions.reshape(rows, 1), inv_row, _pick(rows, 1024))

    xs = _input_ln(x.reshape(rows, d), ln_in_g[None, :], ln_in_b[None, :], _pick(rows, 512))

    for l in range(depth):
        q2, k2, v2, uc, up = _inproj(
            xs, _prep_in_weights(w_in[l], ql, kvl, 2 * cwid), _prep_uq(w_uq[l]), _prep_ukv(w_ukv[l]),
            q_norm_g[l][None, :], kv_norm_g[l][None, :], cos, sin,
            heads=heads, cwid2=2 * cwid, pwid=pwid, tm=_pick(rows, 512))
        ym = _attention(q2.reshape(bsz, seq, -1), k2.reshape(bsz, seq, -1), v2.reshape(bsz, seq, -1),
                        heads=heads, tq=_pick(seq, 256))
        yc, yp = _mixers(uc.reshape(bsz, seq, -1), up.reshape(bsz, seq, -1),
                         conv_w[l], conv_b[l][None, :], conv_ln_g[l][None, :], conv_ln_b[l][None, :],
                         w_pool[l].astype(BF16), pool_scale[l][None, :], tm=_pick(seq, 512))
        xs = _outproj(ym.reshape(rows, -1), yc.reshape(rows, -1), yp.reshape(rows, -1), xs,
                      w_out[l].astype(BF16), ln1_g[l][None, :], ln1_b[l][None, :],
                      alpha=alpha, tm=_pick(rows, 512))
        xs = _ffn(xs, w_up[l].astype(BF16), ffn_conv_w[l], ffn_conv_b[l][None, :], w_down[l].astype(BF16),
                  ln2_g[l][None, :], ln2_b[l][None, :],
                  seq=seq, alpha=alpha, tm=_pick(seq, 512), tf=_pick(w_down.shape[1], 512))
    return xs.reshape(bsz, seq, d)
```

```python
import functools

import jax
import jax.numpy as jnp
from jax import lax
from jax.experimental import pallas as pl
from jax.experimental.pallas import tpu as pltpu

F32 = jnp.float32
BF16 = jnp.bfloat16

QK_NOPE_DIM = 128
QK_ROPE_DIM = 64
V_HEAD_DIM = 128
POOL_WINDOWS = (2, 4, 8, 16)
ROPE_THETA = 10000.0
LN_EPS = 1e-5
RMS_EPS = 1e-6
LOG2_E = 1.4426950408889634

LANES = 128
SUBLANES = 8
VMEM_LIMIT_BYTES = 56 * 1024 * 1024


def _params(*sem):
    return pltpu.CompilerParams(dimension_semantics=sem, vmem_limit_bytes=VMEM_LIMIT_BYTES)


def _resident(shape):
    return pl.BlockSpec(shape, lambda *_: (0,) * len(shape), pipeline_mode=pl.Buffered(1))


def _layer_norm(z, g8, b8):
    mu = jnp.mean(z, axis=-1, keepdims=True)
    zc = z - mu
    var = jnp.mean(zc * zc, axis=-1, keepdims=True)
    y = zc * lax.rsqrt(var + LN_EPS)
    rows, d = y.shape
    return (y.reshape(rows // SUBLANES, SUBLANES, d) * g8 + b8).reshape(rows, d)


def _rows8(v):
    return jnp.broadcast_to(v[None, :], (SUBLANES, v.shape[0]))


def _rms_norm(z, g):
    ms = jnp.mean(z * z, axis=-1, keepdims=True)
    return z * lax.rsqrt(ms + RMS_EPS) * g


def _sigmoid(z):
    return 1.0 / (1.0 + jnp.exp(-z))


def _rope_table_kernel(pos_ref, inv_ref, cos_ref, sin_ref):
    ang = pos_ref[...].astype(F32) * inv_ref[...]
    cos_ref[...] = jnp.cos(ang)
    sin_ref[...] = jnp.sin(ang)


def _rope_table(pos_col, inv_row, tm):
    rows = pos_col.shape[0]
    return pl.pallas_call(
        _rope_table_kernel,
        grid=(rows // tm,),
        in_specs=[pl.BlockSpec((tm, 1), lambda i: (i, 0)),
                  pl.BlockSpec((1, LANES), lambda i: (0, 0))],
        out_specs=[pl.BlockSpec((tm, LANES), lambda i: (i, 0))] * 2,
        out_shape=[jax.ShapeDtypeStruct((rows, LANES), F32)] * 2,
        compiler_params=_params("arbitrary"),
        name="rope_table",
    )(pos_col, inv_row)


def _ln_kernel(x_ref, g_ref, b_ref, o_ref):
    o_ref[...] = _layer_norm(x_ref[...], g_ref[...], b_ref[...])


def _input_ln(x2, g, b, tm):
    rows, d = x2.shape
    return pl.pallas_call(
        _ln_kernel,
        grid=(rows // tm,),
        in_specs=[pl.BlockSpec((tm, d), lambda i: (i, 0)),
                  pl.BlockSpec((SUBLANES, d), lambda i: (0, 0)),
                  pl.BlockSpec((SUBLANES, d), lambda i: (0, 0))],
        out_specs=pl.BlockSpec((tm, d), lambda i: (i, 0)),
        out_shape=jax.ShapeDtypeStruct((rows, d), F32),
        compiler_params=_params("arbitrary"),
        name="input_ln",
    )(x2, g, b)


CONV_HALO = 32
POOL_HALO = 16
CONV_CHUNK = 32


def _inproj_kernel(x_ref, win_ref, wuq_ref, wukv_ref, qg_ref, kvg_ref, cos_ref, sin_ref,
                   cw_ref, cb_ref, lg_ref, lb_ref, wp_ref, ps_ref,
                   q_ref, k_ref, v_ref, yc_ref, yp_ref, hbuf, hsh, ybuf, wrep, pbuf,
                   *, heads, ql, kvl, cwid, taps, tiles_per_seq, scale):
    dn, dv = QK_NOPE_DIM, V_HEAD_DIM
    tm = x_ref.shape[0]
    o_kr = ql + kvl
    o_uc = o_kr + 2 * LANES
    o_up = o_uc + 2 * cwid
    i = pl.program_id(0)
    st = i % tiles_per_seq

    @pl.when(st == 0)
    def _():
        hbuf[0:CONV_HALO, :] = jnp.zeros((CONV_HALO, cwid), F32)
        pbuf[0:POOL_HALO, :] = jnp.zeros((POOL_HALO, pbuf.shape[1]), F32)

    @pl.when(i == 0)
    def _():
        for k in range(taps):
            wrep[k] = jnp.broadcast_to(cw_ref[k:k + 1, :], (SUBLANES, cwid))

    xb = x_ref[...].astype(BF16)
    cos = cos_ref[...]
    sin = sin_ref[...]

    uc = jnp.dot(xb, win_ref[:, o_uc:o_up], preferred_element_type=F32)
    hbuf[CONV_HALO:CONV_HALO + tm, :] = uc[:, :cwid] * _sigmoid(uc[:, cwid:])
    span = tm + CONV_HALO - SUBLANES
    for sh in range(1, SUBLANES):
        hsh[sh - 1, 0:span, :] = hbuf[sh:sh + span, :]

    cq = jnp.dot(xb, win_ref[:, :ql], preferred_element_type=F32)
    ckv = jnp.dot(xb, win_ref[:, ql:o_kr], preferred_element_type=F32)
    hk = jnp.dot(xb, win_ref[:, o_kr:o_uc], preferred_element_type=F32)
    pbuf[POOL_HALO:POOL_HALO + tm, :] = jnp.dot(xb, win_ref[:, o_up:], preferred_element_type=F32)

    base = CONV_HALO - (taps - 1)
    nsub = CONV_CHUNK // SUBLANES

    def conv_chunks(lo, hi):
        for c in range(lo, hi):
            r0 = c * CONV_CHUNK
            for s in range(nsub):
                acc = jnp.zeros((SUBLANES, cwid), F32)
                for k in range(taps):
                    off = base + k
                    sh = off % SUBLANES
                    rows = pl.ds(r0 + (off // SUBLANES + s) * SUBLANES, SUBLANES)
                    acc = acc + wrep[k] * (hbuf[rows, :] if sh == 0 else hsh[sh - 1, rows, :])
                ybuf[pl.ds(r0 + s * SUBLANES, SUBLANES), :] = acc

    nchunk = tm // CONV_CHUNK
    conv_chunks(0, nchunk // 2)

    cqn = _rms_norm(cq, qg_ref[...]).astype(BF16)
    ckvn = _rms_norm(ckv, kvg_ref[...]).astype(BF16)
    kr = (hk[:, :LANES] * cos + hk[:, LANES:] * sin).astype(BF16)

    q = jnp.dot(cqn, wuq_ref[...], preferred_element_type=F32)
    n_nope = heads * dn
    n_rope = heads * QK_ROPE_DIM
    lane = lax.broadcasted_iota(jnp.int32, (1, LANES), 1)
    for p in range(n_rope // LANES):
        a = q[:, n_nope + p * LANES:n_nope + (p + 1) * LANES]
        r = q[:, n_nope + n_rope + p * LANES:n_nope + n_rope + (p + 1) * LANES]
        roped = (a * cos + r * sin) * scale
        for s in range(2):
            h = 2 * p + s
            keep = (lane < QK_ROPE_DIM) if s == 0 else (lane >= QK_ROPE_DIM)
            q_ref[:, h * 2 * dn + dn:(h + 1) * 2 * dn] = jnp.where(keep, roped, 0.0).astype(BF16)
    for h in range(heads):
        q_ref[:, h * 2 * dn:h * 2 * dn + dn] = (q[:, h * dn:(h + 1) * dn] * scale).astype(BF16)

    conv_chunks(nchunk // 2, nchunk)
    hbuf[0:CONV_HALO, :] = hbuf[tm:tm + CONV_HALO, :]

    kv = jnp.dot(ckvn, wukv_ref[...], preferred_element_type=F32)
    for h in range(heads):
        k_ref[:, h * 2 * dn:h * 2 * dn + dn] = kv[:, h * dn:(h + 1) * dn].astype(BF16)
        k_ref[:, h * 2 * dn + dn:(h + 1) * 2 * dn] = kr
    v_ref[...] = kv[:, n_nope:].astype(BF16)

    y = _layer_norm(ybuf[...] + cb_ref[...], lg_ref[...], lb_ref[...])
    yc_ref[...] = (y * _sigmoid(y)).astype(BF16)

    t = st * tm + lax.broadcasted_iota(jnp.int32, (tm, 1), 0)
    for gi, w in enumerate(POOL_WINDOWS):
        lo, hi = gi * LANES, (gi + 1) * LANES
        ws = pbuf[POOL_HALO:POOL_HALO + tm, lo:hi]
        for d in range(1, w):
            ws = ws + pbuf[POOL_HALO - d:POOL_HALO - d + tm, lo:hi]
        cnt = jnp.minimum(t + 1, w).astype(F32)
        dev = ws / cnt - pbuf[POOL_HALO:POOL_HALO + tm, lo:hi]
        yg = jnp.dot(dev.astype(BF16), wp_ref[gi], preferred_element_type=F32)
        yp_ref[:, lo:hi] = (yg * ps_ref[:, lo:hi]).astype(BF16)
    pbuf[0:POOL_HALO, :] = pbuf[tm:tm + POOL_HALO, :]


def _inproj(x2, win, wuq, wukv, qg, kvg, cos, sin, cw, cb, lg, lb, wp, ps, *, heads, seq, tm):
    rows, d = x2.shape
    ql = qg.shape[1]
    kvl = kvg.shape[1]
    taps, cwid = cw.shape
    pwid = ps.shape[1]
    dn, dv = QK_NOPE_DIM, V_HEAD_DIM
    assert taps - 1 <= CONV_HALO and max(POOL_WINDOWS) - 1 <= POOL_HALO
    assert pwid == len(POOL_WINDOWS) * LANES and tm % (2 * CONV_CHUNK) == 0 and seq % tm == 0
    scale = (QK_NOPE_DIM + QK_ROPE_DIM) ** -0.5 * LOG2_E
    row = lambda w: pl.BlockSpec((tm, w), lambda i: (i, 0))
    return pl.pallas_call(
        functools.partial(_inproj_kernel, heads=heads, ql=ql, kvl=kvl, cwid=cwid, taps=taps,
                          tiles_per_seq=seq // tm, scale=scale),
        grid=(rows // tm,),
        in_specs=[row(d), _resident(win.shape), _resident(wuq.shape), _resident(wukv.shape),
                  _resident(qg.shape), _resident(kvg.shape), row(LANES), row(LANES),
                  _resident(cw.shape), _resident(cb.shape), _resident(lg.shape), _resident(lb.shape),
                  _resident(wp.shape), _resident(ps.shape)],
        out_specs=[row(heads * 2 * dn), row(heads * 2 * dn), row(heads * dv), row(cwid), row(pwid)],
        out_shape=[jax.ShapeDtypeStruct((rows, heads * 2 * dn), BF16),
                   jax.ShapeDtypeStruct((rows, heads * 2 * dn), BF16),
                   jax.ShapeDtypeStruct((rows, heads * dv), BF16),
                   jax.ShapeDtypeStruct((rows, cwid), BF16),
                   jax.ShapeDtypeStruct((rows, pwid), BF16)],
        scratch_shapes=[pltpu.VMEM((tm + CONV_HALO, cwid), F32),
                        pltpu.VMEM((SUBLANES - 1, tm + CONV_HALO, cwid), F32),
                        pltpu.VMEM((tm, cwid), F32),
                        pltpu.VMEM((taps, SUBLANES, cwid), F32),
                        pltpu.VMEM((tm + POOL_HALO, pwid), F32)],
        compiler_params=_params("arbitrary"),
        name="inproj",
    )(x2, win, wuq, wukv, qg, kvg, cos, sin, cw, cb, lg, lb, wp, ps)


ATTN_PAD_ROWS = 16


def _attn_kernel(q_ref, k_ref, v_ref, o_ref, vt_ref, m_ref, acc_ref, *, tq):
    seq = q_ref.shape[1]
    nb = seq // tq
    nt = (((1,), (1,)), ((), ()))
    dv = v_ref.shape[2]
    vt_ref[0:dv, :] = v_ref[0].astype(F32).T.astype(BF16)
    ones_row = lax.broadcasted_iota(jnp.int32, (ATTN_PAD_ROWS, seq), 0) == 0
    vt_ref[dv:, :] = jnp.where(ones_row, 1.0, 0.0).astype(BF16)
    key = lax.broadcasted_iota(jnp.int32, (tq, tq), 0)
    qry = lax.broadcasted_iota(jnp.int32, (tq, tq), 1)
    neg = jnp.finfo(F32).min
    for j in range(nb):
        kj = k_ref[0, j * tq:(j + 1) * tq, :]
        s_all = lax.dot_general(kj, q_ref[0, j * tq:, :], nt, preferred_element_type=F32)
        vtj = vt_ref[:, j * tq:(j + 1) * tq]
        for i in range(j, nb):
            cols = slice(i * tq, (i + 1) * tq)
            s = s_all[:, (i - j) * tq:(i - j + 1) * tq]
            if i == j:
                s = jnp.where(key <= qry, s, neg)
            m_new = jnp.max(s, axis=0, keepdims=True)
            if j > 0:
                m_old = m_ref[:, cols]
                m_new = jnp.maximum(m_old, m_new)
            p = jnp.exp2(s - m_new)
            acc = jnp.dot(vtj, p.astype(BF16), preferred_element_type=F32)
            if j > 0:
                acc = jnp.exp2(m_old - m_new) * acc_ref[:, cols] + acc
            if i > j:
                m_ref[:, cols] = m_new
                acc_ref[:, cols] = acc
            else:
                o_ref[0, cols, :] = (acc[0:dv, :] / acc[dv:dv + 1, :]).T.astype(BF16)


def _attention(q3, k3, v3, *, heads, tq):
    bsz, seq, _ = q3.shape
    dqk = 2 * QK_NOPE_DIM
    dv = V_HEAD_DIM
    return pl.pallas_call(
        functools.partial(_attn_kernel, tq=tq),
        grid=(bsz, heads),
        in_specs=[pl.BlockSpec((1, seq, dqk), lambda b, h: (b, 0, h)),
                  pl.BlockSpec((1, seq, dqk), lambda b, h: (b, 0, h)),
                  pl.BlockSpec((1, seq, dv), lambda b, h: (b, 0, h))],
        out_specs=pl.BlockSpec((1, seq, dv), lambda b, h: (b, 0, h)),
        out_shape=jax.ShapeDtypeStruct((bsz, seq, heads * dv), BF16),
        scratch_shapes=[pltpu.VMEM((dv + ATTN_PAD_ROWS, seq), BF16),
                        pltpu.VMEM((1, seq), F32),
                        pltpu.VMEM((dv + ATTN_PAD_ROWS, seq), F32)],
        compiler_params=_params("arbitrary", "arbitrary"),
        name="attention",
    )(q3, k3, v3)


OUTPROJ_SPLIT = 4


def _outproj_kernel(ym_ref, yc_ref, yp_ref, x_ref, w_ref, g_ref, b_ref, o_ref, *, alpha):
    tr = o_ref.shape[0] // OUTPROJ_SPLIT
    for s in range(OUTPROJ_SPLIT):
        rows = slice(s * tr, (s + 1) * tr)
        mixed = jnp.concatenate([ym_ref[rows, :], yc_ref[rows, :], yp_ref[rows, :]], axis=1)
        y = jnp.dot(mixed, w_ref[...], preferred_element_type=F32)
        o_ref[rows, :] = _layer_norm(alpha * x_ref[rows, :] + y, g_ref[...], b_ref[...])


def _outproj(ym, yc, yp, x2, w, g, b, *, alpha, tm):
    rows, d = x2.shape
    row = lambda wd: pl.BlockSpec((tm, wd), lambda i: (i, 0))
    return pl.pallas_call(
        functools.partial(_outproj_kernel, alpha=alpha),
        grid=(rows // tm,),
        in_specs=[row(ym.shape[1]), row(yc.shape[1]), row(yp.shape[1]), row(d),
                  _resident(w.shape), _resident(g.shape), _resident(b.shape)],
        out_specs=row(d),
        out_shape=jax.ShapeDtypeStruct((rows, d), F32),
        compiler_params=_params("arbitrary"),
        name="outproj",
    )(ym, yc, yp, x2, w, g, b)


FFN_HALO = 8
FFN_CHUNK = 64
FFN_HALVES = 2


def _ffn_kernel(x_ref, wa_ref, wg_ref, cwa_ref, cwg_ref, cba_ref, cbg_ref, wd_ref, g_ref, b_ref,
                o_ref, xb_ref, ua_buf, ug_buf, hid0, hid1, halo_a, halo_g, *, tm, tf, tiles_per_seq, alpha):
    i = pl.program_id(0)
    j = pl.program_id(1)
    nj = pl.num_programs(1)

    th = tf // FFN_HALVES

    @pl.when(j == 0)
    def _():
        x = x_ref[...]
        xb_ref[...] = x.astype(BF16)
        o_ref[...] = alpha * x

    @pl.when(jnp.logical_and(i == 0, j == 0))
    def _():
        halo_a[...] = jnp.zeros(halo_a.shape, F32)
        halo_g[...] = jnp.zeros(halo_g.shape, F32)

    keep = (i % tiles_per_seq) != 0
    ua_buf[0:FFN_HALO, :] = jnp.where(keep, halo_a[j], 0.0)
    ug_buf[0:FFN_HALO, :] = jnp.where(keep, halo_g[j], 0.0)

    xb = xb_ref[...]
    hids = (hid0, hid1)
    for h in range(FFN_HALVES):
        cols = slice(h * th, (h + 1) * th)
        ua_buf[FFN_HALO:FFN_HALO + tm, cols] = jnp.dot(xb, wa_ref[:, cols], preferred_element_type=F32)
        ug_buf[FFN_HALO:FFN_HALO + tm, cols] = jnp.dot(xb, wg_ref[:, cols], preferred_element_type=F32)
        cwa, cwg = cwa_ref[:, cols], cwg_ref[:, cols]
        cba, cbg = cba_ref[:, cols], cbg_ref[:, cols]
        for c in range(tm // FFN_CHUNK):
            r = FFN_HALO + c * FFN_CHUNK

            def conv(buf, cw, cb):
                return (cw[2:3, :] * buf[r:r + FFN_CHUNK, cols]
                        + cw[1:2, :] * buf[r - 1:r - 1 + FFN_CHUNK, cols]
                        + cw[0:1, :] * buf[r - 2:r - 2 + FFN_CHUNK, cols] + cb)

            ca = conv(ua_buf, cwa, cba)
            cg = conv(ug_buf, cwg, cbg)
            hids[h][c * FFN_CHUNK:(c + 1) * FFN_CHUNK, :] = (ca * (cg * _sigmoid(cg))).astype(BF16)
    halo_a[j] = ua_buf[tm:tm + FFN_HALO, :]
    halo_g[j] = ug_buf[tm:tm + FFN_HALO, :]
    for h in range(FFN_HALVES):
        o_ref[...] += jnp.dot(hids[h][...], wd_ref[h * th:(h + 1) * th, :], preferred_element_type=F32)

    @pl.when(j == nj - 1)
    def _():
        o_ref[...] = _layer_norm(o_ref[...], g_ref[...], b_ref[...])


def _ffn(x2, wup, cw, cb, wdn, g, b, *, seq, alpha, tm, tf):
    rows, d = x2.shape
    dff = wdn.shape[0]
    nj = dff // tf
    th = tf // FFN_HALVES
    assert seq % tm == 0 and dff % tf == 0 and cw.shape[0] - 1 <= FFN_HALO
    assert FFN_HALVES == 2 and th % LANES == 0 and tm % FFN_CHUNK == 0
    return pl.pallas_call(
        functools.partial(_ffn_kernel, tm=tm, tf=tf, tiles_per_seq=seq // tm, alpha=alpha),
        grid=(rows // tm, nj),
        in_specs=[pl.BlockSpec((tm, d), lambda i, j: (i, 0)),
                  pl.BlockSpec((d, tf), lambda i, j: (0, j)),
                  pl.BlockSpec((d, tf), lambda i, j: (0, j + nj)),
                  pl.BlockSpec((cw.shape[0], tf), lambda i, j: (0, j)),
                  pl.BlockSpec((cw.shape[0], tf), lambda i, j: (0, j + nj)),
                  pl.BlockSpec((1, tf), lambda i, j: (0, j)),
                  pl.BlockSpec((1, tf), lambda i, j: (0, j + nj)),
                  pl.BlockSpec((tf, d), lambda i, j: (j, 0)),
                  _resident(g.shape), _resident(b.shape)],
        out_specs=pl.BlockSpec((tm, d), lambda i, j: (i, 0)),
        out_shape=jax.ShapeDtypeStruct((rows, d), F32),
        scratch_shapes=[pltpu.VMEM((tm, d), BF16),
                        pltpu.VMEM((tm + FFN_HALO, tf), F32),
                        pltpu.VMEM((tm + FFN_HALO, tf), F32),
                        pltpu.VMEM((tm, th), BF16),
                        pltpu.VMEM((tm, th), BF16),
                        pltpu.VMEM((nj, FFN_HALO, tf), F32),
                        pltpu.VMEM((nj, FFN_HALO, tf), F32)],
        compiler_params=_params("arbitrary", "arbitrary"),
        name="ffn",
    )(x2, wup, wup, cw, cw, cb, cb, wdn, g, b)


def _rot_half(w):
    half = w.shape[-1] // 2
    return jnp.concatenate([-w[..., half:], w[..., :half]], axis=-1)


def _prep_in_weights(w_in, ql, kvl):
    o_kr = ql + kvl
    o_uc = o_kr + QK_ROPE_DIM
    kr = w_in[..., o_kr:o_uc]
    krr = _rot_half(kr)
    return jnp.concatenate([w_in[..., :o_kr], kr, kr, krr, krr, w_in[..., o_uc:]], axis=-1).astype(BF16)


def _prep_uq(w_uq):
    depth, r, heads, _ = w_uq.shape
    nope = w_uq[..., :QK_NOPE_DIM].reshape(depth, r, heads * QK_NOPE_DIM)
    rope = w_uq[..., QK_NOPE_DIM:]
    return jnp.concatenate([nope, rope.reshape(depth, r, -1), _rot_half(rope).reshape(depth, r, -1)],
                           axis=-1).astype(BF16)


def _prep_ukv(w_ukv):
    depth, r, heads, _ = w_ukv.shape
    kn = w_ukv[..., :QK_NOPE_DIM].reshape(depth, r, heads * QK_NOPE_DIM)
    v = w_ukv[..., QK_NOPE_DIM:].reshape(depth, r, heads * V_HEAD_DIM)
    return jnp.concatenate([kn, v], axis=-1).astype(BF16)


def _pick(n, pref):
    t = min(n, pref)
    assert n % t == 0
    return t


def kernel(x, positions, ln_in_g, ln_in_b, w_in, q_norm_g, w_uq, kv_norm_g, w_ukv, conv_w, conv_b, conv_ln_g, conv_ln_b, w_pool, pool_scale, w_out, ln1_g, ln1_b, w_up, ffn_conv_w, ffn_conv_b, w_down, ln2_g, ln2_b):
    bsz, seq, d = x.shape
    depth = w_in.shape[0]
    rows = bsz * seq
    heads = w_uq.shape[2]
    ql = q_norm_g.shape[1]
    kvl = kv_norm_g.shape[1]
    alpha = (2.0 * depth) ** 0.25

    inv = 1.0 / (ROPE_THETA ** (jnp.arange(0, QK_ROPE_DIM, 2, dtype=F32) / QK_ROPE_DIM))
    inv_row = jnp.tile(inv, LANES // inv.shape[0])[None, :]
    cos, sin = _rope_table(positions.reshape(rows, 1), inv_row, _pick(rows, 1024))

    xs = _input_ln(x.reshape(rows, d), _rows8(ln_in_g), _rows8(ln_in_b), _pick(rows, 512))

    win_b, wuq_b, wukv_b = _prep_in_weights(w_in, ql, kvl), _prep_uq(w_uq), _prep_ukv(w_ukv)
    wpool_b, wout_b, wup_b, wdown_b = (w.astype(BF16) for w in (w_pool, w_out, w_up, w_down))

    for l in range(depth):
        q2, k2, v2, yc, yp = _inproj(
            xs, win_b[l], wuq_b[l], wukv_b[l], q_norm_g[l][None, :], kv_norm_g[l][None, :], cos, sin,
            conv_w[l], conv_b[l][None, :], _rows8(conv_ln_g[l]), _rows8(conv_ln_b[l]),
            wpool_b[l], pool_scale[l][None, :], heads=heads, seq=seq, tm=_pick(seq, 512))
        ym = _attention(q2.reshape(bsz, seq, -1), k2.reshape(bsz, seq, -1), v2.reshape(bsz, seq, -1),
                        heads=heads, tq=_pick(seq, 256))
        xs = _outproj(ym.reshape(rows, -1), yc, yp, xs,
                      wout_b[l], _rows8(ln1_g[l]), _rows8(ln1_b[l]),
                      alpha=alpha, tm=_pick(rows, 512))
        xs = _ffn(xs, wup_b[l], ffn_conv_w[l], ffn_conv_b[l][None, :], wdown_b[l],
                  _rows8(ln2_g[l]), _rows8(ln2_b[l]),
                  seq=seq, alpha=alpha, tm=_pick(seq, 512), tf=_pick(w_down.shape[1], 512))
    return xs.reshape(bsz, seq, d)
```

```python
import functools

import jax
import jax.numpy as jnp
from jax import lax
from jax.experimental import pallas as pl
from jax.experimental.pallas import tpu as pltpu

F32 = jnp.float32
BF16 = jnp.bfloat16

QK_NOPE_DIM = 128
QK_ROPE_DIM = 64
V_HEAD_DIM = 128
POOL_WINDOWS = (2, 4, 8, 16)
ROPE_THETA = 10000.0
LN_EPS = 1e-5
RMS_EPS = 1e-6
LOG2_E = 1.4426950408889634

LANES = 128
SUBLANES = 8
VMEM_LIMIT_BYTES = 56 * 1024 * 1024


def _params(*sem):
    return pltpu.CompilerParams(dimension_semantics=sem, vmem_limit_bytes=VMEM_LIMIT_BYTES)


def _resident(shape):
    return pl.BlockSpec(shape, lambda *_: (0,) * len(shape), pipeline_mode=pl.Buffered(1))


def _layer_norm(z, g8, b8):
    mu = jnp.mean(z, axis=-1, keepdims=True)
    zc = z - mu
    var = jnp.mean(zc * zc, axis=-1, keepdims=True)
    y = zc * lax.rsqrt(var + LN_EPS)
    rows, d = y.shape
    return (y.reshape(rows // SUBLANES, SUBLANES, d) * g8 + b8).reshape(rows, d)


def _rows8(v):
    return jnp.broadcast_to(v[None, :], (SUBLANES, v.shape[0]))


def _resident_layer(stack, layer):
    tail = stack.shape[1:]
    return pl.BlockSpec((None,) + tail, lambda *_: (layer,) + (0,) * len(tail), pipeline_mode=pl.Buffered(1))


def _rms_norm(z, g):
    ms = jnp.mean(z * z, axis=-1, keepdims=True)
    return z * lax.rsqrt(ms + RMS_EPS) * g


def _sigmoid(z):
    return 1.0 / (1.0 + jnp.exp(-z))


def _rope_table_kernel(pos_ref, inv_ref, cos_ref, sin_ref):
    ang = pos_ref[...].astype(F32) * inv_ref[...]
    cos_ref[...] = jnp.cos(ang)
    sin_ref[...] = jnp.sin(ang)


def _rope_table(pos_col, inv_row, tm):
    rows = pos_col.shape[0]
    return pl.pallas_call(
        _rope_table_kernel,
        grid=(rows // tm,),
        in_specs=[pl.BlockSpec((tm, 1), lambda i: (i, 0)),
                  pl.BlockSpec((1, LANES), lambda i: (0, 0))],
        out_specs=[pl.BlockSpec((tm, LANES), lambda i: (i, 0))] * 2,
        out_shape=[jax.ShapeDtypeStruct((rows, LANES), F32)] * 2,
        compiler_params=_params("arbitrary"),
        name="rope_table",
    )(pos_col, inv_row)


def _ln_kernel(x_ref, g_ref, b_ref, o_ref):
    o_ref[...] = _layer_norm(x_ref[...], g_ref[...], b_ref[...])


def _input_ln(x2, g, b, tm):
    rows, d = x2.shape
    return pl.pallas_call(
        _ln_kernel,
        grid=(rows // tm,),
        in_specs=[pl.BlockSpec((tm, d), lambda i: (i, 0)),
                  pl.BlockSpec((SUBLANES, d), lambda i: (0, 0)),
                  pl.BlockSpec((SUBLANES, d), lambda i: (0, 0))],
        out_specs=pl.BlockSpec((tm, d), lambda i: (i, 0)),
        out_shape=jax.ShapeDtypeStruct((rows, d), F32),
        compiler_params=_params("arbitrary"),
        name="input_ln",
    )(x2, g, b)


CONV_HALO = 32
POOL_HALO = 16
CONV_CHUNK = 32


def _inproj_kernel(x_ref, win_ref, wuq_ref, wukv_ref, qg_ref, kvg_ref, cos_ref, sin_ref,
                   cw_ref, cb_ref, lg_ref, lb_ref, wp_ref, ps_ref,
                   q_ref, k_ref, v_ref, yc_ref, yp_ref, hbuf, hsh, ybuf, wrep, pbuf,
                   *, heads, ql, kvl, cwid, taps, tiles_per_seq, scale):
    dn, dv = QK_NOPE_DIM, V_HEAD_DIM
    tm = x_ref.shape[0]
    o_kr = ql + kvl
    o_uc = o_kr + 2 * LANES
    o_up = o_uc + 2 * cwid
    i = pl.program_id(0)
    st = i % tiles_per_seq

    @pl.when(st == 0)
    def _():
        hbuf[0:CONV_HALO, :] = jnp.zeros((CONV_HALO, cwid), F32)
        pbuf[0:POOL_HALO, :] = jnp.zeros((POOL_HALO, pbuf.shape[1]), F32)

    @pl.when(i == 0)
    def _():
        for k in range(taps):
            wrep[k] = jnp.broadcast_to(cw_ref[k:k + 1, :], (SUBLANES, cwid))

    xb = x_ref[...].astype(BF16)
    cos = cos_ref[...]
    sin = sin_ref[...]

    uc = jnp.dot(xb, win_ref[:, o_uc:o_up], preferred_element_type=F32)
    hbuf[CONV_HALO:CONV_HALO + tm, :] = uc[:, :cwid] * _sigmoid(uc[:, cwid:])
    span = tm + CONV_HALO - SUBLANES
    for sh in range(1, SUBLANES):
        hsh[sh - 1, 0:span, :] = hbuf[sh:sh + span, :]

    cq = jnp.dot(xb, win_ref[:, :ql], preferred_element_type=F32)
    ckv = jnp.dot(xb, win_ref[:, ql:o_kr], preferred_element_type=F32)
    hk = jnp.dot(xb, win_ref[:, o_kr:o_uc], preferred_element_type=F32)
    pbuf[POOL_HALO:POOL_HALO + tm, :] = jnp.dot(xb, win_ref[:, o_up:], preferred_element_type=F32)

    base = CONV_HALO - (taps - 1)
    nsub = CONV_CHUNK // SUBLANES

    def conv_chunks(lo, hi):
        for c in range(lo, hi):
            r0 = c * CONV_CHUNK
            for s in range(nsub):
                acc = jnp.zeros((SUBLANES, cwid), F32)
                for k in range(taps):
                    off = base + k
                    sh = off % SUBLANES
                    rows = pl.ds(r0 + (off // SUBLANES + s) * SUBLANES, SUBLANES)
                    acc = acc + wrep[k] * (hbuf[rows, :] if sh == 0 else hsh[sh - 1, rows, :])
                ybuf[pl.ds(r0 + s * SUBLANES, SUBLANES), :] = acc

    nchunk = tm // CONV_CHUNK
    conv_chunks(0, nchunk // 2)

    cqn = _rms_norm(cq, qg_ref[...]).astype(BF16)
    ckvn = _rms_norm(ckv, kvg_ref[...]).astype(BF16)
    kr = (hk[:, :LANES] * cos + hk[:, LANES:] * sin).astype(BF16)

    q = jnp.dot(cqn, wuq_ref[...], preferred_element_type=F32)
    n_nope = heads * dn
    n_rope = heads * QK_ROPE_DIM
    lane = lax.broadcasted_iota(jnp.int32, (1, LANES), 1)
    for p in range(n_rope // LANES):
        a = q[:, n_nope + p * LANES:n_nope + (p + 1) * LANES]
        r = q[:, n_nope + n_rope + p * LANES:n_nope + n_rope + (p + 1) * LANES]
        roped = (a * cos + r * sin) * scale
        for s in range(2):
            h = 2 * p + s
            keep = (lane < QK_ROPE_DIM) if s == 0 else (lane >= QK_ROPE_DIM)
            q_ref[:, h * 2 * dn + dn:(h + 1) * 2 * dn] = jnp.where(keep, roped, 0.0).astype(BF16)
    for h in range(heads):
        q_ref[:, h * 2 * dn:h * 2 * dn + dn] = (q[:, h * dn:(h + 1) * dn] * scale).astype(BF16)

    conv_chunks(nchunk // 2, nchunk)
    hbuf[0:CONV_HALO, :] = hbuf[tm:tm + CONV_HALO, :]

    kv = jnp.dot(ckvn, wukv_ref[...], preferred_element_type=F32)
    for h in range(heads):
        k_ref[:, h * 2 * dn:h * 2 * dn + dn] = kv[:, h * dn:(h + 1) * dn].astype(BF16)
        k_ref[:, h * 2 * dn + dn:(h + 1) * 2 * dn] = kr
    v_ref[...] = kv[:, n_nope:].astype(BF16)

    y = _layer_norm(ybuf[...] + cb_ref[...], lg_ref[...], lb_ref[...])
    yc_ref[...] = (y * _sigmoid(y)).astype(BF16)

    t = st * tm + lax.broadcasted_iota(jnp.int32, (tm, 1), 0)
    for gi, w in enumerate(POOL_WINDOWS):
        lo, hi = gi * LANES, (gi + 1) * LANES
        ws = pbuf[POOL_HALO:POOL_HALO + tm, lo:hi]
        for d in range(1, w):
            ws = ws + pbuf[POOL_HALO - d:POOL_HALO - d + tm, lo:hi]
        cnt = jnp.minimum(t + 1, w).astype(F32)
        dev = ws / cnt - pbuf[POOL_HALO:POOL_HALO + tm, lo:hi]
        yg = jnp.dot(dev.astype(BF16), wp_ref[gi], preferred_element_type=F32)
        yp_ref[:, lo:hi] = (yg * ps_ref[:, lo:hi]).astype(BF16)
    pbuf[0:POOL_HALO, :] = pbuf[tm:tm + POOL_HALO, :]


def _inproj(x2, win, wuq, wukv, qg, kvg, cos, sin, cw, cb, lg, lb, wp, ps, *, layer, heads, seq, tm):
    rows, d = x2.shape
    ql = qg.shape[1]
    kvl = kvg.shape[1]
    taps, cwid = cw.shape
    pwid = ps.shape[1]
    dn, dv = QK_NOPE_DIM, V_HEAD_DIM
    assert taps - 1 <= CONV_HALO and max(POOL_WINDOWS) - 1 <= POOL_HALO
    assert pwid == len(POOL_WINDOWS) * LANES and tm % (2 * CONV_CHUNK) == 0 and seq % tm == 0
    scale = (QK_NOPE_DIM + QK_ROPE_DIM) ** -0.5 * LOG2_E
    row = lambda w: pl.BlockSpec((tm, w), lambda i: (i, 0))
    return pl.pallas_call(
        functools.partial(_inproj_kernel, heads=heads, ql=ql, kvl=kvl, cwid=cwid, taps=taps,
                          tiles_per_seq=seq // tm, scale=scale),
        grid=(rows // tm,),
        in_specs=[row(d), _resident_layer(win, layer), _resident_layer(wuq, layer), _resident_layer(wukv, layer),
                  _resident(qg.shape), _resident(kvg.shape), row(LANES), row(LANES),
                  _resident(cw.shape), _resident(cb.shape), _resident(lg.shape), _resident(lb.shape),
                  _resident_layer(wp, layer), _resident(ps.shape)],
        out_specs=[row(heads * 2 * dn), row(heads * 2 * dn), row(heads * dv), row(cwid), row(pwid)],
        out_shape=[jax.ShapeDtypeStruct((rows, heads * 2 * dn), BF16),
                   jax.ShapeDtypeStruct((rows, heads * 2 * dn), BF16),
                   jax.ShapeDtypeStruct((rows, heads * dv), BF16),
                   jax.ShapeDtypeStruct((rows, cwid), BF16),
                   jax.ShapeDtypeStruct((rows, pwid), BF16)],
        scratch_shapes=[pltpu.VMEM((tm + CONV_HALO, cwid), F32),
                        pltpu.VMEM((SUBLANES - 1, tm + CONV_HALO, cwid), F32),
                        pltpu.VMEM((tm, cwid), F32),
                        pltpu.VMEM((taps, SUBLANES, cwid), F32),
                        pltpu.VMEM((tm + POOL_HALO, pwid), F32)],
        compiler_params=_params("arbitrary"),
        name="inproj",
    )(x2, win, wuq, wukv, qg, kvg, cos, sin, cw, cb, lg, lb, wp, ps)


ATTN_PAD_ROWS = 16


def _attn_kernel(q_ref, k_ref, v_ref, o_ref, vt_ref, m_ref, acc_ref, *, tq):
    seq = q_ref.shape[1]
    nb = seq // tq
    nt = (((1,), (1,)), ((), ()))
    dv = v_ref.shape[2]
    vt_ref[0:dv, :] = v_ref[0].astype(F32).T.astype(BF16)
    ones_row = lax.broadcasted_iota(jnp.int32, (ATTN_PAD_ROWS, seq), 0) == 0
    vt_ref[dv:, :] = jnp.where(ones_row, 1.0, 0.0).astype(BF16)
    key = lax.broadcasted_iota(jnp.int32, (tq, tq), 0)
    qry = lax.broadcasted_iota(jnp.int32, (tq, tq), 1)
    neg = jnp.finfo(F32).min
    for j in range(nb):
        kj = k_ref[0, j * tq:(j + 1) * tq, :]
        s_all = lax.dot_general(kj, q_ref[0, j * tq:, :], nt, preferred_element_type=F32)
        vtj = vt_ref[:, j * tq:(j + 1) * tq]
        for i in range(j, nb):
            cols = slice(i * tq, (i + 1) * tq)
            s = s_all[:, (i - j) * tq:(i - j + 1) * tq]
            if i == j:
                s = jnp.where(key <= qry, s, neg)
            m_new = jnp.max(s, axis=0, keepdims=True)
            if j > 0:
                m_old = m_ref[:, cols]
                m_new = jnp.maximum(m_old, m_new)
            p = jnp.exp2(s - m_new)
            acc = jnp.dot(vtj, p.astype(BF16), preferred_element_type=F32)
            if j > 0:
                acc = jnp.exp2(m_old - m_new) * acc_ref[:, cols] + acc
            if i > j:
                m_ref[:, cols] = m_new
                acc_ref[:, cols] = acc
            else:
                o_ref[0, cols, :] = (acc[0:dv, :] / acc[dv:dv + 1, :]).T.astype(BF16)


def _attention(q3, k3, v3, *, heads, tq):
    bsz, seq, _ = q3.shape
    dqk = 2 * QK_NOPE_DIM
    dv = V_HEAD_DIM
    return pl.pallas_call(
        functools.partial(_attn_kernel, tq=tq),
        grid=(bsz, heads),
        in_specs=[pl.BlockSpec((1, seq, dqk), lambda b, h: (b, 0, h)),
                  pl.BlockSpec((1, seq, dqk), lambda b, h: (b, 0, h)),
                  pl.BlockSpec((1, seq, dv), lambda b, h: (b, 0, h))],
        out_specs=pl.BlockSpec((1, seq, dv), lambda b, h: (b, 0, h)),
        out_shape=jax.ShapeDtypeStruct((bsz, seq, heads * dv), BF16),
        scratch_shapes=[pltpu.VMEM((dv + ATTN_PAD_ROWS, seq), BF16),
                        pltpu.VMEM((1, seq), F32),
                        pltpu.VMEM((dv + ATTN_PAD_ROWS, seq), F32)],
        compiler_params=_params("arbitrary", "arbitrary"),
        name="attention",
    )(q3, k3, v3)


OUTPROJ_SPLIT = 4


def _outproj_kernel(ym_ref, yc_ref, yp_ref, x_ref, w_ref, g_ref, b_ref, o_ref, *, alpha):
    tr = o_ref.shape[0] // OUTPROJ_SPLIT
    for s in range(OUTPROJ_SPLIT):
        rows = slice(s * tr, (s + 1) * tr)
        mixed = jnp.concatenate([ym_ref[rows, :], yc_ref[rows, :], yp_ref[rows, :]], axis=1)
        y = jnp.dot(mixed, w_ref[...], preferred_element_type=F32)
        o_ref[rows, :] = _layer_norm(alpha * x_ref[rows, :] + y, g_ref[...], b_ref[...])


def _outproj(ym, yc, yp, x2, w, g, b, *, layer, alpha, tm):
    rows, d = x2.shape
    row = lambda wd: pl.BlockSpec((tm, wd), lambda i: (i, 0))
    return pl.pallas_call(
        functools.partial(_outproj_kernel, alpha=alpha),
        grid=(rows // tm,),
        in_specs=[row(ym.shape[1]), row(yc.shape[1]), row(yp.shape[1]), row(d),
                  _resident_layer(w, layer), _resident(g.shape), _resident(b.shape)],
        out_specs=row(d),
        out_shape=jax.ShapeDtypeStruct((rows, d), F32),
        compiler_params=_params("arbitrary"),
        name="outproj",
    )(ym, yc, yp, x2, w, g, b)


FFN_HALO = 8
FFN_CHUNK = 64
FFN_HALVES = 2


def _ffn_kernel(x_ref, wa_ref, wg_ref, cwa_ref, cwg_ref, cba_ref, cbg_ref, wd_ref, g_ref, b_ref,
                o_ref, xb_ref, ua_buf, ug_buf, hid0, hid1, halo_a, halo_g, *, tm, tf, tiles_per_seq, alpha):
    i = pl.program_id(0)
    j = pl.program_id(1)
    nj = pl.num_programs(1)

    th = tf // FFN_HALVES

    @pl.when(j == 0)
    def _():
        x = x_ref[...]
        xb_ref[...] = x.astype(BF16)
        o_ref[...] = alpha * x

    @pl.when(jnp.logical_and(i == 0, j == 0))
    def _():
        halo_a[...] = jnp.zeros(halo_a.shape, F32)
        halo_g[...] = jnp.zeros(halo_g.shape, F32)

    keep = (i % tiles_per_seq) != 0
    ua_buf[0:FFN_HALO, :] = jnp.where(keep, halo_a[j], 0.0)
    ug_buf[0:FFN_HALO, :] = jnp.where(keep, halo_g[j], 0.0)

    xb = xb_ref[...]
    hids = (hid0, hid1)
    for h in range(FFN_HALVES):
        cols = slice(h * th, (h + 1) * th)
        ua_buf[FFN_HALO:FFN_HALO + tm, cols] = jnp.dot(xb, wa_ref[:, cols], preferred_element_type=F32)
        ug_buf[FFN_HALO:FFN_HALO + tm, cols] = jnp.dot(xb, wg_ref[:, cols], preferred_element_type=F32)
        cwa, cwg = cwa_ref[:, cols], cwg_ref[:, cols]
        cba, cbg = cba_ref[:, cols], cbg_ref[:, cols]
        for c in range(tm // FFN_CHUNK):
            r = FFN_HALO + c * FFN_CHUNK

            def conv(buf, cw, cb):
                return (cw[2:3, :] * buf[r:r + FFN_CHUNK, cols]
                        + cw[1:2, :] * buf[r - 1:r - 1 + FFN_CHUNK, cols]
                        + cw[0:1, :] * buf[r - 2:r - 2 + FFN_CHUNK, cols] + cb)

            ca = conv(ua_buf, cwa, cba)
            cg = conv(ug_buf, cwg, cbg)
            hids[h][c * FFN_CHUNK:(c + 1) * FFN_CHUNK, :] = (ca * (cg * _sigmoid(cg))).astype(BF16)
    halo_a[j] = ua_buf[tm:tm + FFN_HALO, :]
    halo_g[j] = ug_buf[tm:tm + FFN_HALO, :]
    for h in range(FFN_HALVES):
        o_ref[...] += jnp.dot(hids[h][...], wd_ref[h * th:(h + 1) * th, :], preferred_element_type=F32)

    @pl.when(j == nj - 1)
    def _():
        o_ref[...] = _layer_norm(o_ref[...], g_ref[...], b_ref[...])


def _ffn(x2, wup, cw, cb, wdn, g, b, *, layer, seq, alpha, tm, tf):
    rows, d = x2.shape
    dff = wdn.shape[1]
    nj = dff // tf
    th = tf // FFN_HALVES
    assert seq % tm == 0 and dff % tf == 0 and cw.shape[0] - 1 <= FFN_HALO
    assert FFN_HALVES == 2 and th % LANES == 0 and tm % FFN_CHUNK == 0
    return pl.pallas_call(
        functools.partial(_ffn_kernel, tm=tm, tf=tf, tiles_per_seq=seq // tm, alpha=alpha),
        grid=(rows // tm, nj),
        in_specs=[pl.BlockSpec((tm, d), lambda i, j: (i, 0)),
                  pl.BlockSpec((None, d, tf), lambda i, j: (layer, 0, j)),
                  pl.BlockSpec((None, d, tf), lambda i, j: (layer, 0, j + nj)),
                  pl.BlockSpec((cw.shape[0], tf), lambda i, j: (0, j)),
                  pl.BlockSpec((cw.shape[0], tf), lambda i, j: (0, j + nj)),
                  pl.BlockSpec((1, tf), lambda i, j: (0, j)),
                  pl.BlockSpec((1, tf), lambda i, j: (0, j + nj)),
                  pl.BlockSpec((None, tf, d), lambda i, j: (layer, j, 0)),
                  _resident(g.shape), _resident(b.shape)],
        out_specs=pl.BlockSpec((tm, d), lambda i, j: (i, 0)),
        out_shape=jax.ShapeDtypeStruct((rows, d), F32),
        scratch_shapes=[pltpu.VMEM((tm, d), BF16),
                        pltpu.VMEM((tm + FFN_HALO, tf), F32),
                        pltpu.VMEM((tm + FFN_HALO, tf), F32),
                        pltpu.VMEM((tm, th), BF16),
                        pltpu.VMEM((tm, th), BF16),
                        pltpu.VMEM((nj, FFN_HALO, tf), F32),
                        pltpu.VMEM((nj, FFN_HALO, tf), F32)],
        compiler_params=_params("arbitrary", "arbitrary"),
        name="ffn",
    )(x2, wup, wup, cw, cw, cb, cb, wdn, g, b)


def _rot_half(w):
    half = w.shape[-1] // 2
    return jnp.concatenate([-w[..., half:], w[..., :half]], axis=-1)


def _prep_in_weights(w_in, ql, kvl):
    o_kr = ql + kvl
    o_uc = o_kr + QK_ROPE_DIM
    kr = w_in[..., o_kr:o_uc]
    krr = _rot_half(kr)
    return jnp.concatenate([w_in[..., :o_kr], kr, kr, krr, krr, w_in[..., o_uc:]], axis=-1).astype(BF16)


def _prep_uq(w_uq):
    depth, r, heads, _ = w_uq.shape
    nope = w_uq[..., :QK_NOPE_DIM].reshape(depth, r, heads * QK_NOPE_DIM)
    rope = w_uq[..., QK_NOPE_DIM:]
    return jnp.concatenate([nope, rope.reshape(depth, r, -1), _rot_half(rope).reshape(depth, r, -1)],
                           axis=-1).astype(BF16)


def _prep_ukv(w_ukv):
    depth, r, heads, _ = w_ukv.shape
    kn = w_ukv[..., :QK_NOPE_DIM].reshape(depth, r, heads * QK_NOPE_DIM)
    v = w_ukv[..., QK_NOPE_DIM:].reshape(depth, r, heads * V_HEAD_DIM)
    return jnp.concatenate([kn, v], axis=-1).astype(BF16)


def _pick(n, pref):
    t = min(n, pref)
    assert n % t == 0
    return t


def kernel(x, positions, ln_in_g, ln_in_b, w_in, q_norm_g, w_uq, kv_norm_g, w_ukv, conv_w, conv_b, conv_ln_g, conv_ln_b, w_pool, pool_scale, w_out, ln1_g, ln1_b, w_up, ffn_conv_w, ffn_conv_b, w_down, ln2_g, ln2_b):
    bsz, seq, d = x.shape
    depth = w_in.shape[0]
    rows = bsz * seq
    heads = w_uq.shape[2]
    ql = q_norm_g.shape[1]
    kvl = kv_norm_g.shape[1]
    alpha = (2.0 * depth) ** 0.25

    inv = 1.0 / (ROPE_THETA ** (jnp.arange(0, QK_ROPE_DIM, 2, dtype=F32) / QK_ROPE_DIM))
    inv_row = jnp.tile(inv, LANES // inv.shape[0])[None, :]
    cos, sin = _rope_table(positions.reshape(rows, 1), inv_row, _pick(rows, 1024))

    xs = _input_ln(x.reshape(rows, d), _rows8(ln_in_g), _rows8(ln_in_b), _pick(rows, 512))

    win_b, wuq_b, wukv_b = _prep_in_weights(w_in, ql, kvl), _prep_uq(w_uq), _prep_ukv(w_ukv)
    wpool_b, wout_b, wup_b, wdown_b = (w.astype(BF16) for w in (w_pool, w_out, w_up, w_down))

    for l in range(depth):
        q2, k2, v2, yc, yp = _inproj(
            xs, win_b, wuq_b, wukv_b, q_norm_g[l][None, :], kv_norm_g[l][None, :], cos, sin,
            conv_w[l], conv_b[l][None, :], _rows8(conv_ln_g[l]), _rows8(conv_ln_b[l]),
            wpool_b, pool_scale[l][None, :], layer=l, heads=heads, seq=seq, tm=_pick(seq, 512))
        ym = _attention(q2.reshape(bsz, seq, -1), k2.reshape(bsz, seq, -1), v2.reshape(bsz, seq, -1),
                        heads=heads, tq=_pick(seq, 256))
        xs = _outproj(ym.reshape(rows, -1), yc, yp, xs,
                      wout_b, _rows8(ln1_g[l]), _rows8(ln1_b[l]),
                      layer=l, alpha=alpha, tm=_pick(rows, 512))
        xs = _ffn(xs, wup_b, ffn_conv_w[l], ffn_conv_b[l][None, :], wdown_b,
                  _rows8(ln2_g[l]), _rows8(ln2_b[l]),
                  layer=l, seq=seq, alpha=alpha, tm=_pick(seq, 512), tf=_pick(w_down.shape[1], 512))
    return xs.reshape(bsz, seq, d)
```

```python
import functools

import jax
import jax.numpy as jnp
from jax import lax
from jax.experimental import pallas as pl
from jax.experimental.pallas import tpu as pltpu

F32 = jnp.float32
BF16 = jnp.bfloat16

QK_NOPE_DIM = 128
QK_ROPE_DIM = 64
V_HEAD_DIM = 128
POOL_WINDOWS = (2, 4, 8, 16)
ROPE_THETA = 10000.0
LN_EPS = 1e-5
RMS_EPS = 1e-6
LOG2_E = 1.4426950408889634

LANES = 128
SUBLANES = 8
VMEM_LIMIT_BYTES = 56 * 1024 * 1024


def _params(*sem):
    return pltpu.CompilerParams(dimension_semantics=sem, vmem_limit_bytes=VMEM_LIMIT_BYTES)


def _resident(shape):
    return pl.BlockSpec(shape, lambda *_: (0,) * len(shape), pipeline_mode=pl.Buffered(1))


def _layer_norm(z, g8, b8):
    mu = jnp.mean(z, axis=-1, keepdims=True)
    zc = z - mu
    var = jnp.mean(zc * zc, axis=-1, keepdims=True)
    y = zc * lax.rsqrt(var + LN_EPS)
    rows, d = y.shape
    return (y.reshape(rows // SUBLANES, SUBLANES, d) * g8 + b8).reshape(rows, d)


def _rows8(v):
    return jnp.broadcast_to(v[None, :], (SUBLANES, v.shape[0]))


def _resident_layer(stack, layer):
    tail = stack.shape[1:]
    return pl.BlockSpec((None,) + tail, lambda *_: (layer,) + (0,) * len(tail), pipeline_mode=pl.Buffered(1))


def _rms_norm(z, g):
    ms = jnp.mean(z * z, axis=-1, keepdims=True)
    return z * lax.rsqrt(ms + RMS_EPS) * g


def _sigmoid(z):
    return 1.0 / (1.0 + jnp.exp(-z))


def _rope_table_kernel(pos_ref, inv_ref, cos_ref, sin_ref):
    ang = pos_ref[...].astype(F32) * inv_ref[...]
    cos_ref[...] = jnp.cos(ang)
    sin_ref[...] = jnp.sin(ang)


def _rope_table(pos_col, inv_row, tm):
    rows = pos_col.shape[0]
    return pl.pallas_call(
        _rope_table_kernel,
        grid=(rows // tm,),
        in_specs=[pl.BlockSpec((tm, 1), lambda i: (i, 0)),
                  pl.BlockSpec((1, LANES), lambda i: (0, 0))],
        out_specs=[pl.BlockSpec((tm, LANES), lambda i: (i, 0))] * 2,
        out_shape=[jax.ShapeDtypeStruct((rows, LANES), F32)] * 2,
        compiler_params=_params("arbitrary"),
        name="rope_table",
    )(pos_col, inv_row)


def _ln_kernel(x_ref, g_ref, b_ref, o_ref):
    o_ref[...] = _layer_norm(x_ref[...], g_ref[...], b_ref[...])


def _input_ln(x2, g, b, tm):
    rows, d = x2.shape
    return pl.pallas_call(
        _ln_kernel,
        grid=(rows // tm,),
        in_specs=[pl.BlockSpec((tm, d), lambda i: (i, 0)),
                  pl.BlockSpec((SUBLANES, d), lambda i: (0, 0)),
                  pl.BlockSpec((SUBLANES, d), lambda i: (0, 0))],
        out_specs=pl.BlockSpec((tm, d), lambda i: (i, 0)),
        out_shape=jax.ShapeDtypeStruct((rows, d), F32),
        compiler_params=_params("arbitrary"),
        name="input_ln",
    )(x2, g, b)


CONV_HALO = 32
POOL_HALO = 16
CONV_CHUNK = 32


def _inproj_kernel(x_ref, win_ref, wuq_ref, wukv_ref, qg_ref, kvg_ref, cos_ref, sin_ref,
                   cw_ref, cb_ref, lg_ref, lb_ref, wp_ref, ps_ref,
                   q_ref, k_ref, v_ref, yc_ref, yp_ref, hbuf, hsh, ybuf, wrep, pbuf,
                   *, heads, ql, kvl, cwid, taps, tiles_per_seq, scale):
    dn, dv = QK_NOPE_DIM, V_HEAD_DIM
    tm = x_ref.shape[0]
    o_kr = ql + kvl
    o_uc = o_kr + 2 * LANES
    o_up = o_uc + 2 * cwid
    i = pl.program_id(0)
    st = i % tiles_per_seq

    @pl.when(st == 0)
    def _():
        hbuf[0:CONV_HALO, :] = jnp.zeros((CONV_HALO, cwid), F32)
        pbuf[0:POOL_HALO, :] = jnp.zeros((POOL_HALO, pbuf.shape[1]), F32)

    @pl.when(i == 0)
    def _():
        for k in range(taps):
            wrep[k] = jnp.broadcast_to(cw_ref[k:k + 1, :], (SUBLANES, cwid))

    xb = x_ref[...].astype(BF16)
    cos = cos_ref[...]
    sin = sin_ref[...]

    uc = jnp.dot(xb, win_ref[:, o_uc:o_up], preferred_element_type=F32)
    hbuf[CONV_HALO:CONV_HALO + tm, :] = uc[:, :cwid] * _sigmoid(uc[:, cwid:])
    span = tm + CONV_HALO - SUBLANES
    for sh in range(1, SUBLANES):
        hsh[sh - 1, 0:span, :] = hbuf[sh:sh + span, :]

    base = CONV_HALO - (taps - 1)
    nsub = CONV_CHUNK // SUBLANES

    def conv_chunks(lo, hi):
        for c in range(lo, hi):
            r0 = c * CONV_CHUNK
            for s in range(nsub):
                acc = jnp.zeros((SUBLANES, cwid), F32)
                for k in range(taps):
                    off = base + k
                    sh = off % SUBLANES
                    rows = pl.ds(r0 + (off // SUBLANES + s) * SUBLANES, SUBLANES)
                    acc = acc + wrep[k] * (hbuf[rows, :] if sh == 0 else hsh[sh - 1, rows, :])
                ybuf[pl.ds(r0 + s * SUBLANES, SUBLANES), :] = acc

    nchunk = tm // CONV_CHUNK
    quarter = nchunk // 4
    cq = jnp.dot(xb, win_ref[:, :ql], preferred_element_type=F32)
    conv_chunks(0, quarter)
    ckv = jnp.dot(xb, win_ref[:, ql:o_kr], preferred_element_type=F32)
    hk = jnp.dot(xb, win_ref[:, o_kr:o_uc], preferred_element_type=F32)
    conv_chunks(quarter, 2 * quarter)
    pbuf[POOL_HALO:POOL_HALO + tm, :] = jnp.dot(xb, win_ref[:, o_up:], preferred_element_type=F32)

    cqn = _rms_norm(cq, qg_ref[...]).astype(BF16)
    ckvn = _rms_norm(ckv, kvg_ref[...]).astype(BF16)
    kr = (hk[:, :LANES] * cos + hk[:, LANES:] * sin).astype(BF16)

    q = jnp.dot(cqn, wuq_ref[...], preferred_element_type=F32)
    conv_chunks(2 * quarter, 3 * quarter)
    n_nope = heads * dn
    n_rope = heads * QK_ROPE_DIM
    lane = lax.broadcasted_iota(jnp.int32, (1, LANES), 1)
    for p in range(n_rope // LANES):
        a = q[:, n_nope + p * LANES:n_nope + (p + 1) * LANES]
        r = q[:, n_nope + n_rope + p * LANES:n_nope + n_rope + (p + 1) * LANES]
        roped = (a * cos + r * sin) * scale
        for s in range(2):
            h = 2 * p + s
            keep = (lane < QK_ROPE_DIM) if s == 0 else (lane >= QK_ROPE_DIM)
            q_ref[:, h * 2 * dn + dn:(h + 1) * 2 * dn] = jnp.where(keep, roped, 0.0).astype(BF16)
    for h in range(heads):
        q_ref[:, h * 2 * dn:h * 2 * dn + dn] = (q[:, h * dn:(h + 1) * dn] * scale).astype(BF16)

    kv = jnp.dot(ckvn, wukv_ref[...], preferred_element_type=F32)
    conv_chunks(3 * quarter, nchunk)
    hbuf[0:CONV_HALO, :] = hbuf[tm:tm + CONV_HALO, :]
    for h in range(heads):
        k_ref[:, h * 2 * dn:h * 2 * dn + dn] = kv[:, h * dn:(h + 1) * dn].astype(BF16)
        k_ref[:, h * 2 * dn + dn:(h + 1) * 2 * dn] = kr
    v_ref[...] = kv[:, n_nope:].astype(BF16)

    y = _layer_norm(ybuf[...] + cb_ref[...], lg_ref[...], lb_ref[...])
    yc_ref[...] = (y * _sigmoid(y)).astype(BF16)

    t = st * tm + lax.broadcasted_iota(jnp.int32, (tm, 1), 0)
    for gi, w in enumerate(POOL_WINDOWS):
        lo, hi = gi * LANES, (gi + 1) * LANES
        ws = pbuf[POOL_HALO:POOL_HALO + tm, lo:hi]
        for d in range(1, w):
            ws = ws + pbuf[POOL_HALO - d:POOL_HALO - d + tm, lo:hi]
        cnt = jnp.minimum(t + 1, w).astype(F32)
        dev = ws / cnt - pbuf[POOL_HALO:POOL_HALO + tm, lo:hi]
        yg = jnp.dot(dev.astype(BF16), wp_ref[gi], preferred_element_type=F32)
        yp_ref[:, lo:hi] = (yg * ps_ref[:, lo:hi]).astype(BF16)
    pbuf[0:POOL_HALO, :] = pbuf[tm:tm + POOL_HALO, :]


def _inproj(x2, win, wuq, wukv, qg, kvg, cos, sin, cw, cb, lg, lb, wp, ps, *, layer, heads, seq, tm):
    rows, d = x2.shape
    ql = qg.shape[1]
    kvl = kvg.shape[1]
    taps, cwid = cw.shape
    pwid = ps.shape[1]
    dn, dv = QK_NOPE_DIM, V_HEAD_DIM
    assert taps - 1 <= CONV_HALO and max(POOL_WINDOWS) - 1 <= POOL_HALO
    assert pwid == len(POOL_WINDOWS) * LANES and tm % (2 * CONV_CHUNK) == 0 and seq % tm == 0
    scale = (QK_NOPE_DIM + QK_ROPE_DIM) ** -0.5 * LOG2_E
    row = lambda w: pl.BlockSpec((tm, w), lambda i: (i, 0))
    return pl.pallas_call(
        functools.partial(_inproj_kernel, heads=heads, ql=ql, kvl=kvl, cwid=cwid, taps=taps,
                          tiles_per_seq=seq // tm, scale=scale),
        grid=(rows // tm,),
        in_specs=[row(d), _resident_layer(win, layer), _resident_layer(wuq, layer), _resident_layer(wukv, layer),
                  _resident(qg.shape), _resident(kvg.shape), row(LANES), row(LANES),
                  _resident(cw.shape), _resident(cb.shape), _resident(lg.shape), _resident(lb.shape),
                  _resident_layer(wp, layer), _resident(ps.shape)],
        out_specs=[row(heads * 2 * dn), row(heads * 2 * dn), row(heads * dv), row(cwid), row(pwid)],
        out_shape=[jax.ShapeDtypeStruct((rows, heads * 2 * dn), BF16),
                   jax.ShapeDtypeStruct((rows, heads * 2 * dn), BF16),
                   jax.ShapeDtypeStruct((rows, heads * dv), BF16),
                   jax.ShapeDtypeStruct((rows, cwid), BF16),
                   jax.ShapeDtypeStruct((rows, pwid), BF16)],
        scratch_shapes=[pltpu.VMEM((tm + CONV_HALO, cwid), F32),
                        pltpu.VMEM((SUBLANES - 1, tm + CONV_HALO, cwid), F32),
                        pltpu.VMEM((tm, cwid), F32),
                        pltpu.VMEM((taps, SUBLANES, cwid), F32),
                        pltpu.VMEM((tm + POOL_HALO, pwid), F32)],
        compiler_params=_params("arbitrary"),
        name="inproj",
    )(x2, win, wuq, wukv, qg, kvg, cos, sin, cw, cb, lg, lb, wp, ps)


ATTN_PAD_ROWS = 16


def _attn_kernel(q_ref, k_ref, v_ref, o_ref, vt_ref, m_ref, acc_ref, *, tq):
    seq = q_ref.shape[1]
    nb = seq // tq
    nt = (((1,), (1,)), ((), ()))
    dv = v_ref.shape[2]
    vt_ref[0:dv, :] = v_ref[0].astype(F32).T.astype(BF16)
    ones_row = lax.broadcasted_iota(jnp.int32, (ATTN_PAD_ROWS, seq), 0) == 0
    vt_ref[dv:, :] = jnp.where(ones_row, 1.0, 0.0).astype(BF16)
    key = lax.broadcasted_iota(jnp.int32, (tq, tq), 0)
    qry = lax.broadcasted_iota(jnp.int32, (tq, tq), 1)
    neg = jnp.finfo(F32).min
    for j in range(nb):
        kj = k_ref[0, j * tq:(j + 1) * tq, :]
        s_all = lax.dot_general(kj, q_ref[0, j * tq:, :], nt, preferred_element_type=F32)
        vtj = vt_ref[:, j * tq:(j + 1) * tq]
        for i in range(j, nb):
            cols = slice(i * tq, (i + 1) * tq)
            s = s_all[:, (i - j) * tq:(i - j + 1) * tq]
            if i == j:
                s = jnp.where(key <= qry, s, neg)
            m_new = jnp.max(s, axis=0, keepdims=True)
            if j > 0:
                m_old = m_ref[:, cols]
                m_new = jnp.maximum(m_old, m_new)
            p = jnp.exp2(s - m_new)
            acc = jnp.dot(vtj, p.astype(BF16), preferred_element_type=F32)
            if j > 0:
                acc = jnp.exp2(m_old - m_new) * acc_ref[:, cols] + acc
            if i > j:
                m_ref[:, cols] = m_new
                acc_ref[:, cols] = acc
            else:
                o_ref[0, cols, :] = (acc[0:dv, :] / acc[dv:dv + 1, :]).T.astype(BF16)


def _attention(q3, k3, v3, *, heads, tq):
    bsz, seq, _ = q3.shape
    dqk = 2 * QK_NOPE_DIM
    dv = V_HEAD_DIM
    return pl.pallas_call(
        functools.partial(_attn_kernel, tq=tq),
        grid=(bsz, heads),
        in_specs=[pl.BlockSpec((1, seq, dqk), lambda b, h: (b, 0, h)),
                  pl.BlockSpec((1, seq, dqk), lambda b, h: (b, 0, h)),
                  pl.BlockSpec((1, seq, dv), lambda b, h: (b, 0, h))],
        out_specs=pl.BlockSpec((1, seq, dv), lambda b, h: (b, 0, h)),
        out_shape=jax.ShapeDtypeStruct((bsz, seq, heads * dv), BF16),
        scratch_shapes=[pltpu.VMEM((dv + ATTN_PAD_ROWS, seq), BF16),
                        pltpu.VMEM((1, seq), F32),
                        pltpu.VMEM((dv + ATTN_PAD_ROWS, seq), F32)],
        compiler_params=_params("arbitrary", "arbitrary"),
        name="attention",
    )(q3, k3, v3)


OUTPROJ_SPLIT = 4


def _outproj_kernel(ym_ref, yc_ref, yp_ref, x_ref, w_ref, g_ref, b_ref, o_ref, *, alpha):
    tr = o_ref.shape[0] // OUTPROJ_SPLIT
    for s in range(OUTPROJ_SPLIT):
        rows = slice(s * tr, (s + 1) * tr)
        mixed = jnp.concatenate([ym_ref[rows, :], yc_ref[rows, :], yp_ref[rows, :]], axis=1)
        y = jnp.dot(mixed, w_ref[...], preferred_element_type=F32)
        o_ref[rows, :] = _layer_norm(alpha * x_ref[rows, :] + y, g_ref[...], b_ref[...])


def _outproj(ym, yc, yp, x2, w, g, b, *, layer, alpha, tm):
    rows, d = x2.shape
    row = lambda wd: pl.BlockSpec((tm, wd), lambda i: (i, 0))
    return pl.pallas_call(
        functools.partial(_outproj_kernel, alpha=alpha),
        grid=(rows // tm,),
        in_specs=[row(ym.shape[1]), row(yc.shape[1]), row(yp.shape[1]), row(d),
                  _resident_layer(w, layer), _resident(g.shape), _resident(b.shape)],
        out_specs=row(d),
        out_shape=jax.ShapeDtypeStruct((rows, d), F32),
        compiler_params=_params("arbitrary"),
        name="outproj",
    )(ym, yc, yp, x2, w, g, b)


FFN_HALO = 8
FFN_CHUNK = 64
FFN_HALVES = 2


def _ffn_kernel(x_ref, wa_ref, wg_ref, cwa_ref, cwg_ref, cba_ref, cbg_ref, wd_ref, g_ref, b_ref,
                o_ref, xb_ref, ua_buf, ug_buf, hid0, hid1, halo_a, halo_g, *, tm, tf, tiles_per_seq, alpha):
    i = pl.program_id(0)
    j = pl.program_id(1)
    nj = pl.num_programs(1)

    th = tf // FFN_HALVES

    @pl.when(j == 0)
    def _():
        x = x_ref[...]
        xb_ref[...] = x.astype(BF16)
        o_ref[...] = alpha * x

    @pl.when(jnp.logical_and(i == 0, j == 0))
    def _():
        halo_a[...] = jnp.zeros(halo_a.shape, F32)
        halo_g[...] = jnp.zeros(halo_g.shape, F32)

    keep = (i % tiles_per_seq) != 0
    ua_buf[0:FFN_HALO, :] = jnp.where(keep, halo_a[j], 0.0)
    ug_buf[0:FFN_HALO, :] = jnp.where(keep, halo_g[j], 0.0)

    xb = xb_ref[...]
    hids = (hid0, hid1)
    for h in range(FFN_HALVES):
        cols = slice(h * th, (h + 1) * th)
        ua_buf[FFN_HALO:FFN_HALO + tm, cols] = jnp.dot(xb, wa_ref[:, cols], preferred_element_type=F32)
        ug_buf[FFN_HALO:FFN_HALO + tm, cols] = jnp.dot(xb, wg_ref[:, cols], preferred_element_type=F32)
        cwa, cwg = cwa_ref[:, cols], cwg_ref[:, cols]
        cba, cbg = cba_ref[:, cols], cbg_ref[:, cols]
        for c in range(tm // FFN_CHUNK):
            r = FFN_HALO + c * FFN_CHUNK

            def conv(buf, cw, cb):
                return (cw[2:3, :] * buf[r:r + FFN_CHUNK, cols]
                        + cw[1:2, :] * buf[r - 1:r - 1 + FFN_CHUNK, cols]
                        + cw[0:1, :] * buf[r - 2:r - 2 + FFN_CHUNK, cols] + cb)

            ca = conv(ua_buf, cwa, cba)
            cg = conv(ug_buf, cwg, cbg)
            hids[h][c * FFN_CHUNK:(c + 1) * FFN_CHUNK, :] = (ca * (cg * _sigmoid(cg))).astype(BF16)
    halo_a[j] = ua_buf[tm:tm + FFN_HALO, :]
    halo_g[j] = ug_buf[tm:tm + FFN_HALO, :]
    for h in range(FFN_HALVES):
        o_ref[...] += jnp.dot(hids[h][...], wd_ref[h * th:(h + 1) * th, :], preferred_element_type=F32)

    @pl.when(j == nj - 1)
    def _():
        o_ref[...] = _layer_norm(o_ref[...], g_ref[...], b_ref[...])


def _ffn(x2, wup, cw, cb, wdn, g, b, *, layer, seq, alpha, tm, tf):
    rows, d = x2.shape
    dff = wdn.shape[1]
    nj = dff // tf
    th = tf // FFN_HALVES
    assert seq % tm == 0 and dff % tf == 0 and cw.shape[0] - 1 <= FFN_HALO
    assert FFN_HALVES == 2 and th % LANES == 0 and tm % FFN_CHUNK == 0
    return pl.pallas_call(
        functools.partial(_ffn_kernel, tm=tm, tf=tf, tiles_per_seq=seq // tm, alpha=alpha),
        grid=(rows // tm, nj),
        in_specs=[pl.BlockSpec((tm, d), lambda i, j: (i, 0)),
                  pl.BlockSpec((None, d, tf), lambda i, j: (layer, 0, j)),
                  pl.BlockSpec((None, d, tf), lambda i, j: (layer, 0, j + nj)),
                  pl.BlockSpec((cw.shape[0], tf), lambda i, j: (0, j)),
                  pl.BlockSpec((cw.shape[0], tf), lambda i, j: (0, j + nj)),
                  pl.BlockSpec((1, tf), lambda i, j: (0, j)),
                  pl.BlockSpec((1, tf), lambda i, j: (0, j + nj)),
                  pl.BlockSpec((None, tf, d), lambda i, j: (layer, j, 0)),
                  _resident(g.shape), _resident(b.shape)],
        out_specs=pl.BlockSpec((tm, d), lambda i, j: (i, 0)),
        out_shape=jax.ShapeDtypeStruct((rows, d), F32),
        scratch_shapes=[pltpu.VMEM((tm, d), BF16),
                        pltpu.VMEM((tm + FFN_HALO, tf), F32),
                        pltpu.VMEM((tm + FFN_HALO, tf), F32),
                        pltpu.VMEM((tm, th), BF16),
                        pltpu.VMEM((tm, th), BF16),
                        pltpu.VMEM((nj, FFN_HALO, tf), F32),
                        pltpu.VMEM((nj, FFN_HALO, tf), F32)],
        compiler_params=_params("arbitrary", "arbitrary"),
        name="ffn",
    )(x2, wup, wup, cw, cw, cb, cb, wdn, g, b)


def _rot_half(w):
    half = w.shape[-1] // 2
    return jnp.concatenate([-w[..., half:], w[..., :half]], axis=-1)


def _prep_in_weights(w_in, ql, kvl):
    o_kr = ql + kvl
    o_uc = o_kr + QK_ROPE_DIM
    w = w_in.astype(BF16)
    kr = w[..., o_kr:o_uc]
    krr = _rot_half(kr)
    return jnp.concatenate([w[..., :o_kr], kr, kr, krr, krr, w[..., o_uc:]], axis=-1)


def _prep_uq(w_uq):
    depth, r, heads, _ = w_uq.shape
    nope = w_uq[..., :QK_NOPE_DIM].reshape(depth, r, heads * QK_NOPE_DIM)
    rope = w_uq[..., QK_NOPE_DIM:]
    return jnp.concatenate([nope, rope.reshape(depth, r, -1), _rot_half(rope).reshape(depth, r, -1)],
                           axis=-1).astype(BF16)


def _prep_ukv(w_ukv):
    depth, r, heads, _ = w_ukv.shape
    kn = w_ukv[..., :QK_NOPE_DIM].reshape(depth, r, heads * QK_NOPE_DIM)
    v = w_ukv[..., QK_NOPE_DIM:].reshape(depth, r, heads * V_HEAD_DIM)
    return jnp.concatenate([kn, v], axis=-1).astype(BF16)


def _pick(n, pref):
    t = min(n, pref)
    assert n % t == 0
    return t


def kernel(x, positions, ln_in_g, ln_in_b, w_in, q_norm_g, w_uq, kv_norm_g, w_ukv, conv_w, conv_b, conv_ln_g, conv_ln_b, w_pool, pool_scale, w_out, ln1_g, ln1_b, w_up, ffn_conv_w, ffn_conv_b, w_down, ln2_g, ln2_b):
    bsz, seq, d = x.shape
    depth = w_in.shape[0]
    rows = bsz * seq
    heads = w_uq.shape[2]
    ql = q_norm_g.shape[1]
    kvl = kv_norm_g.shape[1]
    alpha = (2.0 * depth) ** 0.25

    inv = 1.0 / (ROPE_THETA ** (jnp.arange(0, QK_ROPE_DIM, 2, dtype=F32) / QK_ROPE_DIM))
    inv_row = jnp.tile(inv, LANES // inv.shape[0])[None, :]
    cos, sin = _rope_table(positions.reshape(rows, 1), inv_row, _pick(rows, 1024))

    xs = _input_ln(x.reshape(rows, d), _rows8(ln_in_g), _rows8(ln_in_b), _pick(rows, 512))

    win_b, wuq_b, wukv_b = _prep_in_weights(w_in, ql, kvl), _prep_uq(w_uq), _prep_ukv(w_ukv)
    wpool_b, wout_b, wup_b, wdown_b = (w.astype(BF16) for w in (w_pool, w_out, w_up, w_down))

    for l in range(depth):
        q2, k2, v2, yc, yp = _inproj(
            xs, win_b, wuq_b, wukv_b, q_norm_g[l][None, :], kv_norm_g[l][None, :], cos, sin,
            conv_w[l], conv_b[l][None, :], _rows8(conv_ln_g[l]), _rows8(conv_ln_b[l]),
            wpool_b, pool_scale[l][None, :], layer=l, heads=heads, seq=seq, tm=_pick(seq, 512))
        ym = _attention(q2.reshape(bsz, seq, -1), k2.reshape(bsz, seq, -1), v2.reshape(bsz, seq, -1),
                        heads=heads, tq=_pick(seq, 512))
        xs = _outproj(ym.reshape(rows, -1), yc, yp, xs,
                      wout_b, _rows8(ln1_g[l]), _rows8(ln1_b[l]),
                      layer=l, alpha=alpha, tm=_pick(rows, 512))
        xs = _ffn(xs, wup_b, ffn_conv_w[l], ffn_conv_b[l][None, :], wdown_b,
                  _rows8(ln2_g[l]), _rows8(ln2_b[l]),
                  layer=l, seq=seq, alpha=alpha, tm=_pick(seq, 512), tf=_pick(w_down.shape[1], 512))
    return xs.reshape(bsz, seq, d)
```

```python
import functools

import jax
import jax.numpy as jnp
from jax import lax
from jax.experimental import pallas as pl
from jax.experimental.pallas import tpu as pltpu

F32 = jnp.float32
BF16 = jnp.bfloat16

QK_NOPE_DIM = 128
QK_ROPE_DIM = 64
V_HEAD_DIM = 128
POOL_WINDOWS = (2, 4, 8, 16)
ROPE_THETA = 10000.0
LN_EPS = 1e-5
RMS_EPS = 1e-6
LOG2_E = 1.4426950408889634

LANES = 128
SUBLANES = 8
VMEM_LIMIT_BYTES = 56 * 1024 * 1024


def _params(*sem):
    return pltpu.CompilerParams(dimension_semantics=sem, vmem_limit_bytes=VMEM_LIMIT_BYTES)


def _resident(shape):
    return pl.BlockSpec(shape, lambda *_: (0,) * len(shape), pipeline_mode=pl.Buffered(1))


def _layer_norm(z, g8, b8):
    mu = jnp.mean(z, axis=-1, keepdims=True)
    zc = z - mu
    var = jnp.mean(zc * zc, axis=-1, keepdims=True)
    y = zc * lax.rsqrt(var + LN_EPS)
    rows, d = y.shape
    return (y.reshape(rows // SUBLANES, SUBLANES, d) * g8 + b8).reshape(rows, d)


def _rows8(v):
    return jnp.broadcast_to(v[None, :], (SUBLANES, v.shape[0]))


def _resident_layer(stack, layer):
    tail = stack.shape[1:]
    return pl.BlockSpec((None,) + tail, lambda *_: (layer,) + (0,) * len(tail), pipeline_mode=pl.Buffered(1))


def _rms_norm(z, g):
    ms = jnp.mean(z * z, axis=-1, keepdims=True)
    return z * lax.rsqrt(ms + RMS_EPS) * g


def _sigmoid(z):
    return 0.5 * jnp.tanh(0.5 * z) + 0.5


def _silu(z):
    h = 0.5 * z
    return h + h * jnp.tanh(h)


def _rope_table_kernel(pos_ref, inv_ref, cos_ref, sin_ref):
    ang = pos_ref[...].astype(F32) * inv_ref[...]
    cos_ref[...] = jnp.cos(ang)
    sin_ref[...] = jnp.sin(ang)


def _rope_table(pos_col, inv_row, tm):
    rows = pos_col.shape[0]
    return pl.pallas_call(
        _rope_table_kernel,
        grid=(rows // tm,),
        in_specs=[pl.BlockSpec((tm, 1), lambda i: (i, 0)),
                  pl.BlockSpec((1, LANES), lambda i: (0, 0))],
        out_specs=[pl.BlockSpec((tm, LANES), lambda i: (i, 0))] * 2,
        out_shape=[jax.ShapeDtypeStruct((rows, LANES), F32)] * 2,
        compiler_params=_params("arbitrary"),
        name="rope_table",
    )(pos_col, inv_row)


def _ln_kernel(x_ref, g_ref, b_ref, o_ref):
    o_ref[...] = _layer_norm(x_ref[...], g_ref[...], b_ref[...])


def _input_ln(x2, g, b, tm):
    rows, d = x2.shape
    return pl.pallas_call(
        _ln_kernel,
        grid=(rows // tm,),
        in_specs=[pl.BlockSpec((tm, d), lambda i: (i, 0)),
                  pl.BlockSpec((SUBLANES, d), lambda i: (0, 0)),
                  pl.BlockSpec((SUBLANES, d), lambda i: (0, 0))],
        out_specs=pl.BlockSpec((tm, d), lambda i: (i, 0)),
        out_shape=jax.ShapeDtypeStruct((rows, d), F32),
        compiler_params=_params("arbitrary"),
        name="input_ln",
    )(x2, g, b)


CONV_HALO = 32
POOL_HALO = 16
CONV_CHUNK = 32


def _inproj_kernel(x_ref, win_ref, wuq_ref, wukv_ref, qg_ref, kvg_ref, cos_ref, sin_ref,
                   cw_ref, cb_ref, lg_ref, lb_ref, wp_ref, ps_ref,
                   q_ref, k_ref, v_ref, yc_ref, yp_ref, hbuf, hsh, ybuf, wrep, pbuf,
                   *, heads, ql, kvl, cwid, taps, tiles_per_seq, scale):
    dn, dv = QK_NOPE_DIM, V_HEAD_DIM
    tm = x_ref.shape[0]
    o_kr = ql + kvl
    o_uc = o_kr + 2 * LANES
    o_up = o_uc + 2 * cwid
    i = pl.program_id(0)
    st = i % tiles_per_seq

    @pl.when(st == 0)
    def _():
        hbuf[0:CONV_HALO, :] = jnp.zeros((CONV_HALO, cwid), F32)
        pbuf[0:POOL_HALO, :] = jnp.zeros((POOL_HALO, pbuf.shape[1]), F32)

    @pl.when(i == 0)
    def _():
        for k in range(taps):
            wrep[k] = jnp.broadcast_to(cw_ref[k:k + 1, :], (SUBLANES, cwid))

    xb = x_ref[...].astype(BF16)
    cos = cos_ref[...]
    sin = sin_ref[...]

    uc = jnp.dot(xb, win_ref[:, o_uc:o_up], preferred_element_type=F32)
    hbuf[CONV_HALO:CONV_HALO + tm, :] = uc[:, :cwid] * _sigmoid(uc[:, cwid:])
    span = tm + CONV_HALO - SUBLANES
    for sh in range(1, SUBLANES):
        hsh[sh - 1, 0:span, :] = hbuf[sh:sh + span, :]

    base = CONV_HALO - (taps - 1)
    nsub = CONV_CHUNK // SUBLANES

    def conv_chunks(lo, hi):
        for c in range(lo, hi):
            r0 = c * CONV_CHUNK
            for s in range(nsub):
                acc = jnp.zeros((SUBLANES, cwid), F32)
                for k in range(taps):
                    off = base + k
                    sh = off % SUBLANES
                    rows = pl.ds(r0 + (off // SUBLANES + s) * SUBLANES, SUBLANES)
                    acc = acc + wrep[k] * (hbuf[rows, :] if sh == 0 else hsh[sh - 1, rows, :])
                ybuf[pl.ds(r0 + s * SUBLANES, SUBLANES), :] = acc

    nchunk = tm // CONV_CHUNK
    quarter = nchunk // 4
    cq = jnp.dot(xb, win_ref[:, :ql], preferred_element_type=F32)
    conv_chunks(0, quarter)
    ckv = jnp.dot(xb, win_ref[:, ql:o_kr], preferred_element_type=F32)
    hk = jnp.dot(xb, win_ref[:, o_kr:o_uc], preferred_element_type=F32)
    conv_chunks(quarter, 2 * quarter)
    pbuf[POOL_HALO:POOL_HALO + tm, :] = jnp.dot(xb, win_ref[:, o_up:], preferred_element_type=F32)

    cqn = _rms_norm(cq, qg_ref[...]).astype(BF16)
    ckvn = _rms_norm(ckv, kvg_ref[...]).astype(BF16)
    kr = (hk[:, :LANES] * cos + hk[:, LANES:] * sin).astype(BF16)

    q = jnp.dot(cqn, wuq_ref[...], preferred_element_type=F32)
    conv_chunks(2 * quarter, 3 * quarter)
    n_nope = heads * dn
    n_rope = heads * QK_ROPE_DIM
    lane = lax.broadcasted_iota(jnp.int32, (1, LANES), 1)
    for p in range(n_rope // LANES):
        a = q[:, n_nope + p * LANES:n_nope + (p + 1) * LANES]
        r = q[:, n_nope + n_rope + p * LANES:n_nope + n_rope + (p + 1) * LANES]
        roped = (a * cos + r * sin) * scale
        for s in range(2):
            h = 2 * p + s
            keep = (lane < QK_ROPE_DIM) if s == 0 else (lane >= QK_ROPE_DIM)
            q_ref[:, h * 2 * dn + dn:(h + 1) * 2 * dn] = jnp.where(keep, roped, 0.0).astype(BF16)
    for h in range(heads):
        q_ref[:, h * 2 * dn:h * 2 * dn + dn] = (q[:, h * dn:(h + 1) * dn] * scale).astype(BF16)

    kv = jnp.dot(ckvn, wukv_ref[...], preferred_element_type=F32)
    conv_chunks(3 * quarter, nchunk)
    hbuf[0:CONV_HALO, :] = hbuf[tm:tm + CONV_HALO, :]
    for h in range(heads):
        k_ref[:, h * 2 * dn:h * 2 * dn + dn] = kv[:, h * dn:(h + 1) * dn].astype(BF16)
        k_ref[:, h * 2 * dn + dn:(h + 1) * 2 * dn] = kr
    v_ref[...] = kv[:, n_nope:].astype(BF16)

    y = _layer_norm(ybuf[...] + cb_ref[...], lg_ref[...], lb_ref[...])
    yc_ref[...] = _silu(y).astype(BF16)

    t = st * tm + lax.broadcasted_iota(jnp.int32, (tm, 1), 0)
    for gi, w in enumerate(POOL_WINDOWS):
        lo, hi = gi * LANES, (gi + 1) * LANES
        ws = pbuf[POOL_HALO:POOL_HALO + tm, lo:hi]
        for d in range(1, w):
            ws = ws + pbuf[POOL_HALO - d:POOL_HALO - d + tm, lo:hi]
        cnt = jnp.minimum(t + 1, w).astype(F32)
        dev = ws / cnt - pbuf[POOL_HALO:POOL_HALO + tm, lo:hi]
        yg = jnp.dot(dev.astype(BF16), wp_ref[gi], preferred_element_type=F32)
        yp_ref[:, lo:hi] = (yg * ps_ref[:, lo:hi]).astype(BF16)
    pbuf[0:POOL_HALO, :] = pbuf[tm:tm + POOL_HALO, :]


def _inproj(x2, win, wuq, wukv, qg, kvg, cos, sin, cw, cb, lg, lb, wp, ps, *, layer, heads, seq, tm):
    rows, d = x2.shape
    ql = qg.shape[1]
    kvl = kvg.shape[1]
    taps, cwid = cw.shape
    pwid = ps.shape[1]
    dn, dv = QK_NOPE_DIM, V_HEAD_DIM
    assert taps - 1 <= CONV_HALO and max(POOL_WINDOWS) - 1 <= POOL_HALO
    assert pwid == len(POOL_WINDOWS) * LANES and tm % (2 * CONV_CHUNK) == 0 and seq % tm == 0
    scale = (QK_NOPE_DIM + QK_ROPE_DIM) ** -0.5 * LOG2_E
    row = lambda w: pl.BlockSpec((tm, w), lambda i: (i, 0))
    return pl.pallas_call(
        functools.partial(_inproj_kernel, heads=heads, ql=ql, kvl=kvl, cwid=cwid, taps=taps,
                          tiles_per_seq=seq // tm, scale=scale),
        grid=(rows // tm,),
        in_specs=[row(d), _resident_layer(win, layer), _resident_layer(wuq, layer), _resident_layer(wukv, layer),
                  _resident(qg.shape), _resident(kvg.shape), row(LANES), row(LANES),
                  _resident(cw.shape), _resident(cb.shape), _resident(lg.shape), _resident(lb.shape),
                  _resident_layer(wp, layer), _resident(ps.shape)],
        out_specs=[row(heads * 2 * dn), row(heads * 2 * dn), row(heads * dv), row(cwid), row(pwid)],
        out_shape=[jax.ShapeDtypeStruct((rows, heads * 2 * dn), BF16),
                   jax.ShapeDtypeStruct((rows, heads * 2 * dn), BF16),
                   jax.ShapeDtypeStruct((rows, heads * dv), BF16),
                   jax.ShapeDtypeStruct((rows, cwid), BF16),
                   jax.ShapeDtypeStruct((rows, pwid), BF16)],
        scratch_shapes=[pltpu.VMEM((tm + CONV_HALO, cwid), F32),
                        pltpu.VMEM((SUBLANES - 1, tm + CONV_HALO, cwid), F32),
                        pltpu.VMEM((tm, cwid), F32),
                        pltpu.VMEM((taps, SUBLANES, cwid), F32),
                        pltpu.VMEM((tm + POOL_HALO, pwid), F32)],
        compiler_params=_params("arbitrary"),
        name="inproj",
    )(x2, win, wuq, wukv, qg, kvg, cos, sin, cw, cb, lg, lb, wp, ps)


ATTN_PAD_ROWS = 16


def _attn_kernel(q_ref, k_ref, v_ref, o_ref, vt_ref, m_ref, acc_ref, *, tq):
    seq = q_ref.shape[1]
    nb = seq // tq
    nt = (((1,), (1,)), ((), ()))
    dv = v_ref.shape[2]
    vt_ref[0:dv, :] = v_ref[0].astype(F32).T.astype(BF16)
    ones_row = lax.broadcasted_iota(jnp.int32, (ATTN_PAD_ROWS, seq), 0) == 0
    vt_ref[dv:, :] = jnp.where(ones_row, 1.0, 0.0).astype(BF16)
    key = lax.broadcasted_iota(jnp.int32, (tq, tq), 0)
    qry = lax.broadcasted_iota(jnp.int32, (tq, tq), 1)
    neg = jnp.finfo(F32).min
    for j in range(nb):
        kj = k_ref[0, j * tq:(j + 1) * tq, :]
        s_all = lax.dot_general(kj, q_ref[0, j * tq:, :], nt, preferred_element_type=F32)
        vtj = vt_ref[:, j * tq:(j + 1) * tq]
        for i in range(j, nb):
            cols = slice(i * tq, (i + 1) * tq)
            s = s_all[:, (i - j) * tq:(i - j + 1) * tq]
            if i == j:
                s = jnp.where(key <= qry, s, neg)
            m_new = jnp.max(s, axis=0, keepdims=True)
            if j > 0:
                m_old = m_ref[:, cols]
                m_new = jnp.maximum(m_old, m_new)
            p = jnp.exp2(s - m_new)
            acc = jnp.dot(vtj, p.astype(BF16), preferred_element_type=F32)
            if j > 0:
                acc = jnp.exp2(m_old - m_new) * acc_ref[:, cols] + acc
            if i > j:
                m_ref[:, cols] = m_new
                acc_ref[:, cols] = acc
            else:
                o_ref[0, cols, :] = (acc[0:dv, :] / acc[dv:dv + 1, :]).T.astype(BF16)


def _attention(q3, k3, v3, *, heads, tq):
    bsz, seq, _ = q3.shape
    dqk = 2 * QK_NOPE_DIM
    dv = V_HEAD_DIM
    return pl.pallas_call(
        functools.partial(_attn_kernel, tq=tq),
        grid=(bsz, heads),
        in_specs=[pl.BlockSpec((1, seq, dqk), lambda b, h: (b, 0, h)),
                  pl.BlockSpec((1, seq, dqk), lambda b, h: (b, 0, h)),
                  pl.BlockSpec((1, seq, dv), lambda b, h: (b, 0, h))],
        out_specs=pl.BlockSpec((1, seq, dv), lambda b, h: (b, 0, h)),
        out_shape=jax.ShapeDtypeStruct((bsz, seq, heads * dv), BF16),
        scratch_shapes=[pltpu.VMEM((dv + ATTN_PAD_ROWS, seq), BF16),
                        pltpu.VMEM((1, seq), F32),
                        pltpu.VMEM((dv + ATTN_PAD_ROWS, seq), F32)],
        compiler_params=_params("arbitrary", "arbitrary"),
        name="attention",
    )(q3, k3, v3)


OUTPROJ_SPLIT = 4


def _outproj_kernel(ym_ref, yc_ref, yp_ref, x_ref, w_ref, g_ref, b_ref, o_ref, *, alpha):
    tr = o_ref.shape[0] // OUTPROJ_SPLIT
    for s in range(OUTPROJ_SPLIT):
        rows = slice(s * tr, (s + 1) * tr)
        mixed = jnp.concatenate([ym_ref[rows, :], yc_ref[rows, :], yp_ref[rows, :]], axis=1)
        y = jnp.dot(mixed, w_ref[...], preferred_element_type=F32)
        o_ref[rows, :] = _layer_norm(alpha * x_ref[rows, :] + y, g_ref[...], b_ref[...])


def _outproj(ym, yc, yp, x2, w, g, b, *, layer, alpha, tm):
    rows, d = x2.shape
    row = lambda wd: pl.BlockSpec((tm, wd), lambda i: (i, 0))
    return pl.pallas_call(
        functools.partial(_outproj_kernel, alpha=alpha),
        grid=(rows // tm,),
        in_specs=[row(ym.shape[1]), row(yc.shape[1]), row(yp.shape[1]), row(d),
                  _resident_layer(w, layer), _resident(g.shape), _resident(b.shape)],
        out_specs=row(d),
        out_shape=jax.ShapeDtypeStruct((rows, d), F32),
        compiler_params=_params("arbitrary"),
        name="outproj",
    )(ym, yc, yp, x2, w, g, b)


FFN_HALO = 8
FFN_CHUNK = 128
FFN_HALVES = 2


def _ffn_kernel(x_ref, wa_ref, wg_ref, cwa_ref, cwg_ref, cba_ref, cbg_ref, wd_ref, g_ref, b_ref,
                o_ref, xb_ref, ua_buf, ug_buf, hid0, hid1, halo_a, halo_g, *, tm, tf, tiles_per_seq, alpha):
    i = pl.program_id(0)
    j = pl.program_id(1)
    nj = pl.num_programs(1)

    th = tf // FFN_HALVES

    @pl.when(j == 0)
    def _():
        x = x_ref[...]
        xb_ref[...] = x.astype(BF16)
        o_ref[...] = alpha * x

    @pl.when(jnp.logical_and(i == 0, j == 0))
    def _():
        halo_a[...] = jnp.zeros(halo_a.shape, F32)
        halo_g[...] = jnp.zeros(halo_g.shape, F32)

    keep = (i % tiles_per_seq) != 0
    ua_buf[0:FFN_HALO, :] = jnp.where(keep, halo_a[j], 0.0)
    ug_buf[0:FFN_HALO, :] = jnp.where(keep, halo_g[j], 0.0)

    xb = xb_ref[...]
    hids = (hid0, hid1)
    for h in range(FFN_HALVES):
        cols = slice(h * th, (h + 1) * th)
        ua_buf[FFN_HALO:FFN_HALO + tm, cols] = jnp.dot(xb, wa_ref[:, cols], preferred_element_type=F32)
        ug_buf[FFN_HALO:FFN_HALO + tm, cols] = jnp.dot(xb, wg_ref[:, cols], preferred_element_type=F32)
        cwa, cwg = cwa_ref[:, cols], cwg_ref[:, cols]
        cba, cbg = cba_ref[:, cols], cbg_ref[:, cols]
        for c in range(tm // FFN_CHUNK):
            r = FFN_HALO + c * FFN_CHUNK

            def conv(buf, cw, cb):
                return (cw[2:3, :] * buf[r:r + FFN_CHUNK, cols]
                        + cw[1:2, :] * buf[r - 1:r - 1 + FFN_CHUNK, cols]
                        + cw[0:1, :] * buf[r - 2:r - 2 + FFN_CHUNK, cols] + cb)

            ca = conv(ua_buf, cwa, cba)
            cg = conv(ug_buf, cwg, cbg)
            hids[h][c * FFN_CHUNK:(c + 1) * FFN_CHUNK, :] = (ca * _silu(cg)).astype(BF16)
    halo_a[j] = ua_buf[tm:tm + FFN_HALO, :]
    halo_g[j] = ug_buf[tm:tm + FFN_HALO, :]
    for h in range(FFN_HALVES):
        o_ref[...] += jnp.dot(hids[h][...], wd_ref[h * th:(h + 1) * th, :], preferred_element_type=F32)

    @pl.when(j == nj - 1)
    def _():
        o_ref[...] = _layer_norm(o_ref[...], g_ref[...], b_ref[...])


def _ffn(x2, wup, cw, cb, wdn, g, b, *, layer, seq, alpha, tm, tf):
    rows, d = x2.shape
    dff = wdn.shape[1]
    nj = dff // tf
    th = tf // FFN_HALVES
    assert seq % tm == 0 and dff % tf == 0 and cw.shape[0] - 1 <= FFN_HALO
    assert FFN_HALVES == 2 and th % LANES == 0 and tm % FFN_CHUNK == 0
    return pl.pallas_call(
        functools.partial(_ffn_kernel, tm=tm, tf=tf, tiles_per_seq=seq // tm, alpha=alpha),
        grid=(rows // tm, nj),
        in_specs=[pl.BlockSpec((tm, d), lambda i, j: (i, 0)),
                  pl.BlockSpec((None, d, tf), lambda i, j: (layer, 0, j)),
                  pl.BlockSpec((None, d, tf), lambda i, j: (layer, 0, j + nj)),
                  pl.BlockSpec((cw.shape[0], tf), lambda i, j: (0, j)),
                  pl.BlockSpec((cw.shape[0], tf), lambda i, j: (0, j + nj)),
                  pl.BlockSpec((1, tf), lambda i, j: (0, j)),
                  pl.BlockSpec((1, tf), lambda i, j: (0, j + nj)),
                  pl.BlockSpec((None, tf, d), lambda i, j: (layer, j, 0)),
                  _resident(g.shape), _resident(b.shape)],
        out_specs=pl.BlockSpec((tm, d), lambda i, j: (i, 0)),
        out_shape=jax.ShapeDtypeStruct((rows, d), F32),
        scratch_shapes=[pltpu.VMEM((tm, d), BF16),
                        pltpu.VMEM((tm + FFN_HALO, tf), F32),
                        pltpu.VMEM((tm + FFN_HALO, tf), F32),
                        pltpu.VMEM((tm, th), BF16),
                        pltpu.VMEM((tm, th), BF16),
                        pltpu.VMEM((nj, FFN_HALO, tf), F32),
                        pltpu.VMEM((nj, FFN_HALO, tf), F32)],
        compiler_params=_params("arbitrary", "arbitrary"),
        name="ffn",
    )(x2, wup, wup, cw, cw, cb, cb, wdn, g, b)


def _rot_half(w):
    half = w.shape[-1] // 2
    return jnp.concatenate([-w[..., half:], w[..., :half]], axis=-1)


def _prep_in_weights(w_in, ql, kvl):
    o_kr = ql + kvl
    o_uc = o_kr + QK_ROPE_DIM
    w = w_in.astype(BF16)
    kr = w[..., o_kr:o_uc]
    krr = _rot_half(kr)
    return jnp.concatenate([w[..., :o_kr], kr, kr, krr, krr, w[..., o_uc:]], axis=-1)


def _prep_uq(w_uq):
    depth, r, heads, _ = w_uq.shape
    nope = w_uq[..., :QK_NOPE_DIM].reshape(depth, r, heads * QK_NOPE_DIM)
    rope = w_uq[..., QK_NOPE_DIM:]
    return jnp.concatenate([nope, rope.reshape(depth, r, -1), _rot_half(rope).reshape(depth, r, -1)],
                           axis=-1).astype(BF16)


def _prep_ukv(w_ukv):
    depth, r, heads, _ = w_ukv.shape
    kn = w_ukv[..., :QK_NOPE_DIM].reshape(depth, r, heads * QK_NOPE_DIM)
    v = w_ukv[..., QK_NOPE_DIM:].reshape(depth, r, heads * V_HEAD_DIM)
    return jnp.concatenate([kn, v], axis=-1).astype(BF16)


def _pick(n, pref):
    t = min(n, pref)
    assert n % t == 0
    return t


def kernel(x, positions, ln_in_g, ln_in_b, w_in, q_norm_g, w_uq, kv_norm_g, w_ukv, conv_w, conv_b, conv_ln_g, conv_ln_b, w_pool, pool_scale, w_out, ln1_g, ln1_b, w_up, ffn_conv_w, ffn_conv_b, w_down, ln2_g, ln2_b):
    bsz, seq, d = x.shape
    depth = w_in.shape[0]
    rows = bsz * seq
    heads = w_uq.shape[2]
    ql = q_norm_g.shape[1]
    kvl = kv_norm_g.shape[1]
    alpha = (2.0 * depth) ** 0.25

    inv = 1.0 / (ROPE_THETA ** (jnp.arange(0, QK_ROPE_DIM, 2, dtype=F32) / QK_ROPE_DIM))
    inv_row = jnp.tile(inv, LANES // inv.shape[0])[None, :]
    cos, sin = _rope_table(positions.reshape(rows, 1), inv_row, _pick(rows, 1024))

    xs = _input_ln(x.reshape(rows, d), _rows8(ln_in_g), _rows8(ln_in_b), _pick(rows, 512))

    win_b, wuq_b, wukv_b = _prep_in_weights(w_in, ql, kvl), _prep_uq(w_uq), _prep_ukv(w_ukv)
    wpool_b, wout_b, wup_b, wdown_b = (w.astype(BF16) for w in (w_pool, w_out, w_up, w_down))

    for l in range(depth):
        q2, k2, v2, yc, yp = _inproj(
            xs, win_b, wuq_b, wukv_b, q_norm_g[l][None, :], kv_norm_g[l][None, :], cos, sin,
            conv_w[l], conv_b[l][None, :], _rows8(conv_ln_g[l]), _rows8(conv_ln_b[l]),
            wpool_b, pool_scale[l][None, :], layer=l, heads=heads, seq=seq, tm=_pick(seq, 512))
        ym = _attention(q2.reshape(bsz, seq, -1), k2.reshape(bsz, seq, -1), v2.reshape(bsz, seq, -1),
                        heads=heads, tq=_pick(seq, 512))
        xs = _outproj(ym.reshape(rows, -1), yc, yp, xs,
                      wout_b, _rows8(ln1_g[l]), _rows8(ln1_b[l]),
                      layer=l, alpha=alpha, tm=_pick(rows, 512))
        xs = _ffn(xs, wup_b, ffn_conv_w[l], ffn_conv_b[l][None, :], wdown_b,
                  _rows8(ln2_g[l]), _rows8(ln2_b[l]),
                  layer=l, seq=seq, alpha=alpha, tm=_pick(seq, 512), tf=_pick(w_down.shape[1], 512))
    return xs.reshape(bsz, seq, d)
```

```python
import functools

import jax
import jax.numpy as jnp
from jax import lax
from jax.experimental import pallas as pl
from jax.experimental.pallas import tpu as pltpu

F32 = jnp.float32
BF16 = jnp.bfloat16

QK_NOPE_DIM = 128
QK_ROPE_DIM = 64
V_HEAD_DIM = 128
POOL_WINDOWS = (2, 4, 8, 16)
ROPE_THETA = 10000.0
LN_EPS = 1e-5
RMS_EPS = 1e-6
LOG2_E = 1.4426950408889634

LANES = 128
SUBLANES = 8
VMEM_LIMIT_BYTES = 56 * 1024 * 1024


def _params(*sem):
    return pltpu.CompilerParams(dimension_semantics=sem, vmem_limit_bytes=VMEM_LIMIT_BYTES)


def _resident(shape):
    return pl.BlockSpec(shape, lambda *_: (0,) * len(shape), pipeline_mode=pl.Buffered(1))


def _layer_norm(z, g8, b8):
    mu = jnp.mean(z, axis=-1, keepdims=True)
    zc = z - mu
    var = jnp.mean(zc * zc, axis=-1, keepdims=True)
    y = zc * lax.rsqrt(var + LN_EPS)
    rows, d = y.shape
    return (y.reshape(rows // SUBLANES, SUBLANES, d) * g8 + b8).reshape(rows, d)


def _rows8(v):
    return jnp.broadcast_to(v[None, :], (SUBLANES, v.shape[0]))


def _resident_layer(stack, layer):
    tail = stack.shape[1:]
    return pl.BlockSpec((None,) + tail, lambda *_: (layer,) + (0,) * len(tail), pipeline_mode=pl.Buffered(1))


def _rms_norm(z, g):
    ms = jnp.mean(z * z, axis=-1, keepdims=True)
    return z * lax.rsqrt(ms + RMS_EPS) * g


def _sigmoid(z):
    return 0.5 * jnp.tanh(0.5 * z) + 0.5


def _silu(z):
    h = 0.5 * z
    return h + h * jnp.tanh(h)


def _rope_table_kernel(pos_ref, inv_ref, cos_ref, sin_ref):
    ang = pos_ref[...].astype(F32) * inv_ref[...]
    cos_ref[...] = jnp.cos(ang)
    sin_ref[...] = jnp.sin(ang)


def _rope_table(pos_col, inv_row, tm):
    rows = pos_col.shape[0]
    return pl.pallas_call(
        _rope_table_kernel,
        grid=(rows // tm,),
        in_specs=[pl.BlockSpec((tm, 1), lambda i: (i, 0)),
                  pl.BlockSpec((1, LANES), lambda i: (0, 0))],
        out_specs=[pl.BlockSpec((tm, LANES), lambda i: (i, 0))] * 2,
        out_shape=[jax.ShapeDtypeStruct((rows, LANES), F32)] * 2,
        compiler_params=_params("arbitrary"),
        name="rope_table",
    )(pos_col, inv_row)


def _ln_kernel(x_ref, g_ref, b_ref, o_ref):
    o_ref[...] = _layer_norm(x_ref[...], g_ref[...], b_ref[...])


def _input_ln(x2, g, b, tm):
    rows, d = x2.shape
    return pl.pallas_call(
        _ln_kernel,
        grid=(rows // tm,),
        in_specs=[pl.BlockSpec((tm, d), lambda i: (i, 0)),
                  pl.BlockSpec((SUBLANES, d), lambda i: (0, 0)),
                  pl.BlockSpec((SUBLANES, d), lambda i: (0, 0))],
        out_specs=pl.BlockSpec((tm, d), lambda i: (i, 0)),
        out_shape=jax.ShapeDtypeStruct((rows, d), F32),
        compiler_params=_params("arbitrary"),
        name="input_ln",
    )(x2, g, b)


CONV_HALO = 32
POOL_HALO = 16
CONV_CHUNK = 32


def _inproj_kernel(x_ref, win_ref, wuq_ref, wukv_ref, qg_ref, kvg_ref, cos_ref, sin_ref,
                   cw_ref, cb_ref, lg_ref, lb_ref, wp_ref, ps_ref,
                   q_ref, k_ref, v_ref, yc_ref, yp_ref, hbuf, hsh, ybuf, wrep, pbuf,
                   *, heads, ql, kvl, cwid, taps, tiles_per_seq, scale):
    dn, dv = QK_NOPE_DIM, V_HEAD_DIM
    tm = x_ref.shape[0]
    o_kr = ql + kvl
    o_uc = o_kr + 2 * LANES
    o_up = o_uc + 2 * cwid
    i = pl.program_id(0)
    st = i % tiles_per_seq

    @pl.when(st == 0)
    def _():
        hbuf[0:CONV_HALO, :] = jnp.zeros((CONV_HALO, cwid), F32)
        pbuf[0:POOL_HALO, :] = jnp.zeros((POOL_HALO, pbuf.shape[1]), F32)

    @pl.when(i == 0)
    def _():
        for k in range(taps):
            wrep[k] = jnp.broadcast_to(cw_ref[k:k + 1, :], (SUBLANES, cwid))

    xb = x_ref[...].astype(BF16)
    cos = cos_ref[...]
    sin = sin_ref[...]

    uc = jnp.dot(xb, win_ref[:, o_uc:o_up], preferred_element_type=F32)
    hbuf[CONV_HALO:CONV_HALO + tm, :] = uc[:, :cwid] * _sigmoid(uc[:, cwid:])
    span = tm + CONV_HALO - SUBLANES
    for sh in range(1, SUBLANES):
        hsh[sh - 1, 0:span, :] = hbuf[sh:sh + span, :]

    base = CONV_HALO - (taps - 1)
    nsub = CONV_CHUNK // SUBLANES

    def conv_chunks(lo, hi):
        for c in range(lo, hi):
            r0 = c * CONV_CHUNK
            for s in range(nsub):
                acc = jnp.zeros((SUBLANES, cwid), F32)
                for k in range(taps):
                    off = base + k
                    sh = off % SUBLANES
                    rows = pl.ds(r0 + (off // SUBLANES + s) * SUBLANES, SUBLANES)
                    acc = acc + wrep[k] * (hbuf[rows, :] if sh == 0 else hsh[sh - 1, rows, :])
                ybuf[pl.ds(r0 + s * SUBLANES, SUBLANES), :] = acc

    nchunk = tm // CONV_CHUNK
    quarter = nchunk // 4
    cq = jnp.dot(xb, win_ref[:, :ql], preferred_element_type=F32)
    conv_chunks(0, quarter)
    ckv = jnp.dot(xb, win_ref[:, ql:o_kr], preferred_element_type=F32)
    hk = jnp.dot(xb, win_ref[:, o_kr:o_uc], preferred_element_type=F32)
    conv_chunks(quarter, 2 * quarter)
    pbuf[POOL_HALO:POOL_HALO + tm, :] = jnp.dot(xb, win_ref[:, o_up:], preferred_element_type=F32)

    cqn = _rms_norm(cq, qg_ref[...]).astype(BF16)
    ckvn = _rms_norm(ckv, kvg_ref[...]).astype(BF16)
    kr = (hk[:, :LANES] * cos + hk[:, LANES:] * sin).astype(BF16)

    q = jnp.dot(cqn, wuq_ref[...], preferred_element_type=F32)
    conv_chunks(2 * quarter, 3 * quarter)
    n_nope = heads * dn
    n_rope = heads * QK_ROPE_DIM
    lane = lax.broadcasted_iota(jnp.int32, (1, LANES), 1)
    for p in range(n_rope // LANES):
        a = q[:, n_nope + p * LANES:n_nope + (p + 1) * LANES]
        r = q[:, n_nope + n_rope + p * LANES:n_nope + n_rope + (p + 1) * LANES]
        roped = (a * cos + r * sin) * scale
        for s in range(2):
            h = 2 * p + s
            keep = (lane < QK_ROPE_DIM) if s == 0 else (lane >= QK_ROPE_DIM)
            q_ref[:, h * 2 * dn + dn:(h + 1) * 2 * dn] = jnp.where(keep, roped, 0.0).astype(BF16)
    for h in range(heads):
        q_ref[:, h * 2 * dn:h * 2 * dn + dn] = (q[:, h * dn:(h + 1) * dn] * scale).astype(BF16)

    kv = jnp.dot(ckvn, wukv_ref[...], preferred_element_type=F32)
    conv_chunks(3 * quarter, nchunk)
    hbuf[0:CONV_HALO, :] = hbuf[tm:tm + CONV_HALO, :]
    for h in range(heads):
        k_ref[:, h * 2 * dn:h * 2 * dn + dn] = kv[:, h * dn:(h + 1) * dn].astype(BF16)
        k_ref[:, h * 2 * dn + dn:(h + 1) * 2 * dn] = kr
    v_ref[...] = kv[:, n_nope:].astype(BF16)

    y = _layer_norm(ybuf[...] + cb_ref[...], lg_ref[...], lb_ref[...])
    yc_ref[...] = _silu(y).astype(BF16)

    t = st * tm + lax.broadcasted_iota(jnp.int32, (tm, 1), 0)
    for gi, w in enumerate(POOL_WINDOWS):
        lo, hi = gi * LANES, (gi + 1) * LANES
        ws = pbuf[:, lo:hi]
        n = 1
        while n < w:
            ws = ws + pltpu.roll(ws, n, axis=0)
            n *= 2
        ws = ws[POOL_HALO:, :]
        cnt = jnp.minimum(t + 1, w).astype(F32)
        dev = ws / cnt - pbuf[POOL_HALO:POOL_HALO + tm, lo:hi]
        yg = jnp.dot(dev.astype(BF16), wp_ref[gi], preferred_element_type=F32)
        yp_ref[:, lo:hi] = (yg * ps_ref[:, lo:hi]).astype(BF16)
    pbuf[0:POOL_HALO, :] = pbuf[tm:tm + POOL_HALO, :]


def _inproj(x2, win, wuq, wukv, qg, kvg, cos, sin, cw, cb, lg, lb, wp, ps, *, layer, heads, seq, tm):
    rows, d = x2.shape
    ql = qg.shape[1]
    kvl = kvg.shape[1]
    taps, cwid = cw.shape
    pwid = ps.shape[1]
    dn, dv = QK_NOPE_DIM, V_HEAD_DIM
    assert taps - 1 <= CONV_HALO and max(POOL_WINDOWS) - 1 <= POOL_HALO
    assert pwid == len(POOL_WINDOWS) * LANES and tm % (2 * CONV_CHUNK) == 0 and seq % tm == 0
    assert all(w & (w - 1) == 0 for w in POOL_WINDOWS)
    scale = (QK_NOPE_DIM + QK_ROPE_DIM) ** -0.5 * LOG2_E
    row = lambda w: pl.BlockSpec((tm, w), lambda i: (i, 0))
    return pl.pallas_call(
        functools.partial(_inproj_kernel, heads=heads, ql=ql, kvl=kvl, cwid=cwid, taps=taps,
                          tiles_per_seq=seq // tm, scale=scale),
        grid=(rows // tm,),
        in_specs=[row(d), _resident_layer(win, layer), _resident_layer(wuq, layer), _resident_layer(wukv, layer),
                  _resident(qg.shape), _resident(kvg.shape), row(LANES), row(LANES),
                  _resident(cw.shape), _resident(cb.shape), _resident(lg.shape), _resident(lb.shape),
                  _resident_layer(wp, layer), _resident(ps.shape)],
        out_specs=[row(heads * 2 * dn), row(heads * 2 * dn), row(heads * dv), row(cwid), row(pwid)],
        out_shape=[jax.ShapeDtypeStruct((rows, heads * 2 * dn), BF16),
                   jax.ShapeDtypeStruct((rows, heads * 2 * dn), BF16),
                   jax.ShapeDtypeStruct((rows, heads * dv), BF16),
                   jax.ShapeDtypeStruct((rows, cwid), BF16),
                   jax.ShapeDtypeStruct((rows, pwid), BF16)],
        scratch_shapes=[pltpu.VMEM((tm + CONV_HALO, cwid), F32),
                        pltpu.VMEM((SUBLANES - 1, tm + CONV_HALO, cwid), F32),
                        pltpu.VMEM((tm, cwid), F32),
                        pltpu.VMEM((taps, SUBLANES, cwid), F32),
                        pltpu.VMEM((tm + POOL_HALO, pwid), F32)],
        compiler_params=_params("arbitrary"),
        name="inproj",
    )(x2, win, wuq, wukv, qg, kvg, cos, sin, cw, cb, lg, lb, wp, ps)


ATTN_PAD_ROWS = 16


def _attn_kernel(q_ref, k_ref, v_ref, o_ref, vt_ref, m_ref, acc_ref, *, tq):
    seq = q_ref.shape[1]
    nb = seq // tq
    nt = (((1,), (1,)), ((), ()))
    dv = v_ref.shape[2]
    vt_ref[0:dv, :] = v_ref[0].astype(F32).T.astype(BF16)
    ones_row = lax.broadcasted_iota(jnp.int32, (ATTN_PAD_ROWS, seq), 0) == 0
    vt_ref[dv:, :] = jnp.where(ones_row, 1.0, 0.0).astype(BF16)
    key = lax.broadcasted_iota(jnp.int32, (tq, tq), 0)
    qry = lax.broadcasted_iota(jnp.int32, (tq, tq), 1)
    neg = jnp.finfo(F32).min
    for j in range(nb):
        kj = k_ref[0, j * tq:(j + 1) * tq, :]
        s_all = lax.dot_general(kj, q_ref[0, j * tq:, :], nt, preferred_element_type=F32)
        vtj = vt_ref[:, j * tq:(j + 1) * tq]
        for i in range(j, nb):
            cols = slice(i * tq, (i + 1) * tq)
            s = s_all[:, (i - j) * tq:(i - j + 1) * tq]
            if i == j:
                s = jnp.where(key <= qry, s, neg)
            m_new = jnp.max(s, axis=0, keepdims=True)
            if j > 0:
                m_old = m_ref[:, cols]
                m_new = jnp.maximum(m_old, m_new)
            p = jnp.exp2(s - m_new)
            acc = jnp.dot(vtj, p.astype(BF16), preferred_element_type=F32)
            if j > 0:
                acc = jnp.exp2(m_old - m_new) * acc_ref[:, cols] + acc
            if i > j:
                m_ref[:, cols] = m_new
                acc_ref[:, cols] = acc
            else:
                o_ref[0, cols, :] = (acc[0:dv, :] / acc[dv:dv + 1, :]).T.astype(BF16)


def _attention(q3, k3, v3, *, heads, tq):
    bsz, seq, _ = q3.shape
    dqk = 2 * QK_NOPE_DIM
    dv = V_HEAD_DIM
    return pl.pallas_call(
        functools.partial(_attn_kernel, tq=tq),
        grid=(bsz, heads),
        in_specs=[pl.BlockSpec((1, seq, dqk), lambda b, h: (b, 0, h)),
                  pl.BlockSpec((1, seq, dqk), lambda b, h: (b, 0, h)),
                  pl.BlockSpec((1, seq, dv), lambda b, h: (b, 0, h))],
        out_specs=pl.BlockSpec((1, seq, dv), lambda b, h: (b, 0, h)),
        out_shape=jax.ShapeDtypeStruct((bsz, seq, heads * dv), BF16),
        scratch_shapes=[pltpu.VMEM((dv + ATTN_PAD_ROWS, seq), BF16),
                        pltpu.VMEM((1, seq), F32),
                        pltpu.VMEM((dv + ATTN_PAD_ROWS, seq), F32)],
        compiler_params=_params("arbitrary", "arbitrary"),
        name="attention",
    )(q3, k3, v3)


OUTPROJ_SPLIT = 4


def _outproj_kernel(ym_ref, yc_ref, yp_ref, x_ref, w_ref, g_ref, b_ref, o_ref, *, alpha):
    tr = o_ref.shape[0] // OUTPROJ_SPLIT
    for s in range(OUTPROJ_SPLIT):
        rows = slice(s * tr, (s + 1) * tr)
        mixed = jnp.concatenate([ym_ref[rows, :], yc_ref[rows, :], yp_ref[rows, :]], axis=1)
        y = jnp.dot(mixed, w_ref[...], preferred_element_type=F32)
        o_ref[rows, :] = _layer_norm(alpha * x_ref[rows, :] + y, g_ref[...], b_ref[...])


def _outproj(ym, yc, yp, x2, w, g, b, *, layer, alpha, tm):
    rows, d = x2.shape
    row = lambda wd: pl.BlockSpec((tm, wd), lambda i: (i, 0))
    return pl.pallas_call(
        functools.partial(_outproj_kernel, alpha=alpha),
        grid=(rows // tm,),
        in_specs=[row(ym.shape[1]), row(yc.shape[1]), row(yp.shape[1]), row(d),
                  _resident_layer(w, layer), _resident(g.shape), _resident(b.shape)],
        out_specs=row(d),
        out_shape=jax.ShapeDtypeStruct((rows, d), F32),
        compiler_params=_params("arbitrary"),
        name="outproj",
    )(ym, yc, yp, x2, w, g, b)


FFN_HALO = 8
FFN_CHUNK = 128
FFN_HALVES = 2


def _ffn_kernel(x_ref, wa_ref, wg_ref, cwa_ref, cwg_ref, cba_ref, cbg_ref, wd_ref, g_ref, b_ref,
                o_ref, xb_ref, ua_buf, ug_buf, hid0, hid1, halo_a, halo_g, *, tm, tf, tiles_per_seq, alpha):
    i = pl.program_id(0)
    j = pl.program_id(1)
    nj = pl.num_programs(1)

    th = tf // FFN_HALVES

    @pl.when(j == 0)
    def _():
        x = x_ref[...]
        xb_ref[...] = x.astype(BF16)
        o_ref[...] = alpha * x

    @pl.when(jnp.logical_and(i == 0, j == 0))
    def _():
        halo_a[...] = jnp.zeros(halo_a.shape, F32)
        halo_g[...] = jnp.zeros(halo_g.shape, F32)

    keep = (i % tiles_per_seq) != 0
    ua_buf[0:FFN_HALO, :] = jnp.where(keep, halo_a[j], 0.0)
    ug_buf[0:FFN_HALO, :] = jnp.where(keep, halo_g[j], 0.0)

    xb = xb_ref[...]
    hids = (hid0, hid1)
    for h in range(FFN_HALVES):
        cols = slice(h * th, (h + 1) * th)
        ua_buf[FFN_HALO:FFN_HALO + tm, cols] = jnp.dot(xb, wa_ref[:, cols], preferred_element_type=F32)
        ug_buf[FFN_HALO:FFN_HALO + tm, cols] = jnp.dot(xb, wg_ref[:, cols], preferred_element_type=F32)
        cwa, cwg = cwa_ref[:, cols], cwg_ref[:, cols]
        cba, cbg = cba_ref[:, cols], cbg_ref[:, cols]
        for c in range(tm // FFN_CHUNK):
            r = FFN_HALO + c * FFN_CHUNK

            def conv(buf, cw, cb):
                return (cw[2:3, :] * buf[r:r + FFN_CHUNK, cols]
                        + cw[1:2, :] * buf[r - 1:r - 1 + FFN_CHUNK, cols]
                        + cw[0:1, :] * buf[r - 2:r - 2 + FFN_CHUNK, cols] + cb)

            ca = conv(ua_buf, cwa, cba)
            cg = conv(ug_buf, cwg, cbg)
            hids[h][c * FFN_CHUNK:(c + 1) * FFN_CHUNK, :] = (ca * _silu(cg)).astype(BF16)
    halo_a[j] = ua_buf[tm:tm + FFN_HALO, :]
    halo_g[j] = ug_buf[tm:tm + FFN_HALO, :]
    for h in range(FFN_HALVES):
        o_ref[...] += jnp.dot(hids[h][...], wd_ref[h * th:(h + 1) * th, :], preferred_element_type=F32)

    @pl.when(j == nj - 1)
    def _():
        o_ref[...] = _layer_norm(o_ref[...], g_ref[...], b_ref[...])


def _ffn(x2, wup, cw, cb, wdn, g, b, *, layer, seq, alpha, tm, tf):
    rows, d = x2.shape
    dff = wdn.shape[1]
    nj = dff // tf
    th = tf // FFN_HALVES
    assert seq % tm == 0 and dff % tf == 0 and cw.shape[0] - 1 <= FFN_HALO
    assert FFN_HALVES == 2 and th % LANES == 0 and tm % FFN_CHUNK == 0
    return pl.pallas_call(
        functools.partial(_ffn_kernel, tm=tm, tf=tf, tiles_per_seq=seq // tm, alpha=alpha),
        grid=(rows // tm, nj),
        in_specs=[pl.BlockSpec((tm, d), lambda i, j: (i, 0)),
                  pl.BlockSpec((None, d, tf), lambda i, j: (layer, 0, j)),
                  pl.BlockSpec((None, d, tf), lambda i, j: (layer, 0, j + nj)),
                  pl.BlockSpec((cw.shape[0], tf), lambda i, j: (0, j)),
                  pl.BlockSpec((cw.shape[0], tf), lambda i, j: (0, j + nj)),
                  pl.BlockSpec((1, tf), lambda i, j: (0, j)),
                  pl.BlockSpec((1, tf), lambda i, j: (0, j + nj)),
                  pl.BlockSpec((None, tf, d), lambda i, j: (layer, j, 0)),
                  _resident(g.shape), _resident(b.shape)],
        out_specs=pl.BlockSpec((tm, d), lambda i, j: (i, 0)),
        out_shape=jax.ShapeDtypeStruct((rows, d), F32),
        scratch_shapes=[pltpu.VMEM((tm, d), BF16),
                        pltpu.VMEM((tm + FFN_HALO, tf), F32),
                        pltpu.VMEM((tm + FFN_HALO, tf), F32),
                        pltpu.VMEM((tm, th), BF16),
                        pltpu.VMEM((tm, th), BF16),
                        pltpu.VMEM((nj, FFN_HALO, tf), F32),
                        pltpu.VMEM((nj, FFN_HALO, tf), F32)],
        compiler_params=_params("arbitrary", "arbitrary"),
        name="ffn",
    )(x2, wup, wup, cw, cw, cb, cb, wdn, g, b)


def _rot_half(w):
    half = w.shape[-1] // 2
    return jnp.concatenate([-w[..., half:], w[..., :half]], axis=-1)


def _prep_in_weights(w_in, ql, kvl):
    o_kr = ql + kvl
    o_uc = o_kr + QK_ROPE_DIM
    w = w_in.astype(BF16)
    kr = w[..., o_kr:o_uc]
    krr = _rot_half(kr)
    return jnp.concatenate([w[..., :o_kr], kr, kr, krr, krr, w[..., o_uc:]], axis=-1)


def _prep_uq(w_uq):
    depth, r, heads, _ = w_uq.shape
    nope = w_uq[..., :QK_NOPE_DIM].reshape(depth, r, heads * QK_NOPE_DIM)
    rope = w_uq[..., QK_NOPE_DIM:]
    return jnp.concatenate([nope, rope.reshape(depth, r, -1), _rot_half(rope).reshape(depth, r, -1)],
                           axis=-1).astype(BF16)


def _prep_ukv(w_ukv):
    depth, r, heads, _ = w_ukv.shape
    kn = w_ukv[..., :QK_NOPE_DIM].reshape(depth, r, heads * QK_NOPE_DIM)
    v = w_ukv[..., QK_NOPE_DIM:].reshape(depth, r, heads * V_HEAD_DIM)
    return jnp.concatenate([kn, v], axis=-1).astype(BF16)


def _pick(n, pref):
    t = min(n, pref)
    assert n % t == 0
    return t


def kernel(x, positions, ln_in_g, ln_in_b, w_in, q_norm_g, w_uq, kv_norm_g, w_ukv, conv_w, conv_b, conv_ln_g, conv_ln_b, w_pool, pool_scale, w_out, ln1_g, ln1_b, w_up, ffn_conv_w, ffn_conv_b, w_down, ln2_g, ln2_b):
    bsz, seq, d = x.shape
    depth = w_in.shape[0]
    rows = bsz * seq
    heads = w_uq.shape[2]
    ql = q_norm_g.shape[1]
    kvl = kv_norm_g.shape[1]
    alpha = (2.0 * depth) ** 0.25

    inv = 1.0 / (ROPE_THETA ** (jnp.arange(0, QK_ROPE_DIM, 2, dtype=F32) / QK_ROPE_DIM))
    inv_row = jnp.tile(inv, LANES // inv.shape[0])[None, :]
    cos, sin = _rope_table(positions.reshape(rows, 1), inv_row, _pick(rows, 1024))

    xs = _input_ln(x.reshape(rows, d), _rows8(ln_in_g), _rows8(ln_in_b), _pick(rows, 512))

    win_b, wuq_b, wukv_b = _prep_in_weights(w_in, ql, kvl), _prep_uq(w_uq), _prep_ukv(w_ukv)
    wpool_b, wout_b, wup_b, wdown_b = (w.astype(BF16) for w in (w_pool, w_out, w_up, w_down))

    for l in range(depth):
        q2, k2, v2, yc, yp = _inproj(
            xs, win_b, wuq_b, wukv_b, q_norm_g[l][None, :], kv_norm_g[l][None, :], cos, sin,
            conv_w[l], conv_b[l][None, :], _rows8(conv_ln_g[l]), _rows8(conv_ln_b[l]),
            wpool_b, pool_scale[l][None, :], layer=l, heads=heads, seq=seq, tm=_pick(seq, 512))
        ym = _attention(q2.reshape(bsz, seq, -1), k2.reshape(bsz, seq, -1), v2.reshape(bsz, seq, -1),
                        heads=heads, tq=_pick(seq, 512))
        xs = _outproj(ym.reshape(rows, -1), yc, yp, xs,
                      wout_b, _rows8(ln1_g[l]), _rows8(ln1_b[l]),
                      layer=l, alpha=alpha, tm=_pick(rows, 512))
        xs = _ffn(xs, wup_b, ffn_conv_w[l], ffn_conv_b[l][None, :], wdown_b,
                  _rows8(ln2_g[l]), _rows8(ln2_b[l]),
                  layer=l, seq=seq, alpha=alpha, tm=_pick(seq, 512), tf=_pick(w_down.shape[1], 512))
    return xs.reshape(bsz, seq, d)
```
